```python
import jax, jax.numpy as jnp
from jax import lax
import numpy as np

D_MODEL = 2048
BATCH = 4
SEQ = 4096
DEPTH = 2

GRID_W = 64
CTX_LEN = 256
N_MOD = 9
D_FF = 5632
W_RG = 2048
RG_BLOCKS = 16
RG_BW = W_RG // RG_BLOCKS
RG_CONV = 4
RG_PAD_LO = 2
RG_C = 8.0
W_SC = 2048
SC_CONV = 3
SC_PAD_LO = 1
LN_EPS = 1e-5
ALPHA = (2 * DEPTH) ** 0.25
BETA = (8 * DEPTH) ** -0.25
PROJ_SIZES = (W_RG, W_RG, W_SC, W_SC, W_SC, D_MODEL, D_MODEL)
PROJ_W = sum(PROJ_SIZES)
PROJ_SPLITS = tuple(int(v) for v in np.cumsum(PROJ_SIZES)[:-1])

kernel_name = "hybrid_rglru_shortconv_macaron_deepnorm_dit"


def layer_norm(x, g, b):
    xf = x.astype(jnp.float32)
    mu = jnp.mean(xf, axis=-1, keepdims=True)
    var = jnp.mean(jnp.square(xf - mu), axis=-1, keepdims=True)
    return ((xf - mu) * lax.rsqrt(var + LN_EPS) * g.astype(jnp.float32) + b.astype(jnp.float32)).astype(x.dtype)


def modulate(x, shift, scale):
    return x * (1 + scale) + shift


def swiglu(h, w_in, w_out):
    g, u = jnp.split(h @ w_in, 2, axis=-1)
    return (jax.nn.silu(g) * u) @ w_out


def ffn_sublayer(x, shift, scale, gate, w_in, w_out, ln_g, ln_b):
    y = swiglu(modulate(x, shift, scale), w_in, w_out)
    return layer_norm(ALPHA * x + 0.5 * gate * y, ln_g, ln_b)


def dwconv(u, w, pad_lo):
    k_w = w.shape[0]
    t = u.shape[-2]
    pad = [(0, 0)] * (u.ndim - 2) + [(pad_lo, k_w - 1 - pad_lo), (0, 0)]
    up = jnp.pad(u, pad)
    return sum(up[..., k:k + t, :] * w[k] for k in range(k_w))


def conv_latent(u, w, pad_lo):
    b, s, ch = u.shape
    rows = s // GRID_W
    return dwconv(u.reshape(b, rows, GRID_W, ch), w, pad_lo).reshape(b, s, ch)


def _lin_combine(e1, e2):
    a1, v1 = e1
    a2, v2 = e2
    return a1 * a2, a2 * v1 + v2


def rglru_scan(u, w_gate, b_gate, lam, h0, reverse):
    b, t, w = u.shape
    ub = u.reshape(b, t, RG_BLOCKS, RG_BW)
    logits = jnp.einsum("bthi,ghij->gbthj", ub, w_gate).reshape(2, b, t, w) + b_gate[:, None, None, :]
    gates = jax.nn.sigmoid(logits)
    r, i = gates[0], gates[1]
    log_a = RG_C * r * jax.nn.log_sigmoid(lam)
    a = jnp.exp(log_a)
    v = jnp.sqrt(-jnp.expm1(2 * log_a)) * (i * u)
    entry = -1 if reverse else 0
    final = 0 if reverse else -1
    v = v.at[:, entry].add(a[:, entry] * h0)
    _, h = lax.associative_scan(_lin_combine, (a, v), reverse=reverse, axis=1)
    return h, h[:, final]


def rg_branch(u_x, conv, rg_conv_w, rg_conv_b, rg_gate_w, rg_gate_b, rg_lam, h0_f, h0_b):
    xc = conv(u_x, rg_conv_w, RG_PAD_LO) + rg_conv_b
    h_f, hT_f = rglru_scan(xc, rg_gate_w[0], rg_gate_b[0], rg_lam[0], h0_f, reverse=False)
    h_b, hT_b = rglru_scan(xc, rg_gate_w[1], rg_gate_b[1], rg_lam[1], h0_b, reverse=True)
    return h_f + h_b, hT_f, hT_b


def mixer_out(proj, h_rg, conv, sc_conv_w, w_rg_out, w_sc_out, b_merge, w_o):
    _, rg_gate, sc_b, sc_c, sc_x, g_rg, g_sc = jnp.split(proj, PROJ_SPLITS, axis=-1)
    y_rg = (h_rg * jax.nn.gelu(rg_gate)) @ w_rg_out
    y_sc = (sc_b * conv(sc_c * sc_x, sc_conv_w, SC_PAD_LO)) @ w_sc_out
    merged = jax.nn.sigmoid(g_rg + b_merge[0]) * y_rg + jax.nn.sigmoid(g_sc + b_merge[1]) * y_sc
    return merged @ w_o


def setup_inputs(seed: int = 0) -> dict:
    key = jax.random.key(seed)
    ks = jax.random.split(key, 24)

    def nrm(k, shape, scale):
        return jax.random.normal(k, shape, jnp.float32) * scale

    u = jax.random.uniform(ks[17], (DEPTH, 2, W_RG), jnp.float32, 0.81, 0.998)
    a0 = u ** (1.0 / RG_C)
    rg_lam = jnp.log(a0) - jnp.log1p(-a0)
    return {
        "x": nrm(ks[0], (BATCH, SEQ, D_MODEL), 1.0),
        "c": nrm(ks[1], (BATCH, D_MODEL), 1.0),
        "ctx": nrm(ks[2], (BATCH, CTX_LEN, D_MODEL), 1.0),
        "c_ctx": nrm(ks[3], (D_MODEL,), 1.0),
        "w_mod": nrm(ks[4], (DEPTH, D_MODEL, N_MOD * D_MODEL), 0.5 * D_MODEL ** -0.5),
        "b_mod": nrm(ks[5], (DEPTH, N_MOD * D_MODEL), 0.02),
        "ln_g": 1.0 + nrm(ks[6], (DEPTH, 3, D_MODEL), 0.02),
        "ln_b": nrm(ks[7], (DEPTH, 3, D_MODEL), 0.02),
        "ffn1_w_in": nrm(ks[8], (DEPTH, D_MODEL, 2 * D_FF), D_MODEL ** -0.5),
        "ffn1_w_out": nrm(ks[9], (DEPTH, D_FF, D_MODEL), BETA * D_FF ** -0.5),
        "ffn2_w_in": nrm(ks[10], (DEPTH, D_MODEL, 2 * D_FF), D_MODEL ** -0.5),
        "ffn2_w_out": nrm(ks[11], (DEPTH, D_FF, D_MODEL), BETA * D_FF ** -0.5),
        "w_in": nrm(ks[12], (DEPTH, D_MODEL, PROJ_W), D_MODEL ** -0.5),
        "rg_conv_w": nrm(ks[13], (DEPTH, RG_CONV, W_RG), RG_CONV ** -0.5),
        "rg_conv_b": nrm(ks[14], (DEPTH, W_RG), 0.02),
        "rg_gate_w": nrm(ks[15], (DEPTH, 2, 2, RG_BLOCKS, RG_BW, RG_BW), RG_BW ** -0.5),
        "rg_gate_b": nrm(ks[16], (DEPTH, 2, 2, W_RG), 0.02),
        "rg_lam": rg_lam,
        "sc_conv_w": nrm(ks[18], (DEPTH, SC_CONV, W_SC), SC_CONV ** -0.5),
        "w_rg_out": nrm(ks[19], (DEPTH, W_RG, D_MODEL), W_RG ** -0.5),
        "w_sc_out": nrm(ks[20], (DEPTH, W_SC, D_MODEL), W_SC ** -0.5),
        "b_merge": nrm(ks[21], (DEPTH, 2, D_MODEL), 0.02),
        "w_o": nrm(ks[22], (DEPTH, D_MODEL, D_MODEL), BETA * D_MODEL ** -0.5),
    }


def reference(x, c, ctx, c_ctx, w_mod, b_mod, ln_g, ln_b, ffn1_w_in, ffn1_w_out, ffn2_w_in, ffn2_w_out,
              w_in, rg_conv_w, rg_conv_b, rg_gate_w, rg_gate_b, rg_lam, sc_conv_w, w_rg_out, w_sc_out,
              b_merge, w_o):
    xc = ctx
    batch = x.shape[0]
    zeros_state = jnp.zeros((batch, W_RG), x.dtype)
    for l in range(DEPTH):
        last = l == DEPTH - 1
        mod_lat = jnp.split((jax.nn.silu(c) @ w_mod[l] + b_mod[l])[:, None, :], N_MOD, axis=-1)
        mod_ctx = jnp.split(jax.nn.silu(c_ctx) @ w_mod[l] + b_mod[l], N_MOD, axis=-1)

        x = ffn_sublayer(x, mod_lat[0], mod_lat[1], mod_lat[2], ffn1_w_in[l], ffn1_w_out[l], ln_g[l, 0], ln_b[l, 0])
        xc = ffn_sublayer(xc, mod_ctx[0], mod_ctx[1], mod_ctx[2], ffn1_w_in[l], ffn1_w_out[l], ln_g[l, 0], ln_b[l, 0])

        hc = modulate(xc, mod_ctx[3], mod_ctx[4])
        if last:
            pc_rg = hc @ w_in[l][:, :W_RG]
        else:
            pc = hc @ w_in[l]
            pc_rg = pc[..., :W_RG]
        h_rg_c, s_f, s_b = rg_branch(pc_rg, dwconv, rg_conv_w[l], rg_conv_b[l], rg_gate_w[l], rg_gate_b[l],
                                     rg_lam[l], zeros_state, zeros_state)
        if not last:
            yc = mixer_out(pc, h_rg_c, dwconv, sc_conv_w[l], w_rg_out[l], w_sc_out[l], b_merge[l], w_o[l])
            xc = layer_norm(ALPHA * xc + mod_ctx[5] * yc, ln_g[l, 1], ln_b[l, 1])

        hl = modulate(x, mod_lat[3], mod_lat[4])
        pl = hl @ w_in[l]
        h_rg_l, _, _ = rg_branch(pl[..., :W_RG], conv_latent, rg_conv_w[l], rg_conv_b[l], rg_gate_w[l],
                                 rg_gate_b[l], rg_lam[l], s_f, s_b)
        yl = mixer_out(pl, h_rg_l, conv_latent, sc_conv_w[l], w_rg_out[l], w_sc_out[l], b_merge[l], w_o[l])
        x = layer_norm(ALPHA * x + mod_lat[5] * yl, ln_g[l, 1], ln_b[l, 1])

        x = ffn_sublayer(x, mod_lat[6], mod_lat[7], mod_lat[8], ffn2_w_in[l], ffn2_w_out[l], ln_g[l, 2], ln_b[l, 2])
        if not last:
            xc = ffn_sublayer(xc, mod_ctx[6], mod_ctx[7], mod_ctx[8], ffn2_w_in[l], ffn2_w_out[l],
                              ln_g[l, 2], ln_b[l, 2])
    return x
```

```python
import functools
import math

import jax
import jax.numpy as jnp
from jax import lax
from jax.experimental import pallas as pl
from jax.experimental.pallas import tpu as pltpu

F32 = jnp.float32
BF16 = jnp.bfloat16

SUBLANES = 8
LANES = 128
VMEM_BYTES_V7X = 64 * 1024 * 1024
VMEM_LIMIT = VMEM_BYTES_V7X - 8 * 1024 * 1024

GRID_W = 64
N_MOD = 9
RG_BLOCKS = 16
RG_CONV = 4
RG_PAD_LO = 2
SC_CONV = 3
SC_PAD_LO = 1
RG_C = 8.0
LN_EPS = 1e-5
N_PROJ = 7

SEG = GRID_W
ROW_TILE = SEG * SUBLANES
LAT_CHUNKS = SUBLANES
CTX_CHUNKS = 4


def _sigmoid(x):
    return 1.0 / (1.0 + jnp.exp(-x))


def _gelu_tanh(x):
    return x * (0.5 * (1.0 + jnp.tanh(math.sqrt(2.0 / math.pi) * (x + 0.044715 * (x * x * x)))))


def _layer_norm(z, g, b):
    mu = jnp.mean(z, axis=-1, keepdims=True)
    zc = z - mu
    var = jnp.mean(zc * zc, axis=-1, keepdims=True)
    return zc * lax.rsqrt(var + LN_EPS) * g + b


def _params(*sem):
    return pltpu.CompilerParams(dimension_semantics=sem, vmem_limit_bytes=VMEM_LIMIT)


def _mod_kernel(c_ref, w_ref, b_ref, o_ref):
    c = c_ref[...]
    s = (c * _sigmoid(c)).astype(BF16)
    o_ref[0] = jnp.dot(s, w_ref[0].astype(BF16), preferred_element_type=F32) + b_ref[0]


def _mod_vectors(cond, w_mod, b_mod):
    depth, d, n = w_mod.shape
    tn = 1024
    return pl.pallas_call(
        _mod_kernel,
        grid=(depth, n // tn),
        in_specs=[
            pl.BlockSpec((SUBLANES, d), lambda l, j: (0, 0)),
            pl.BlockSpec((1, d, tn), lambda l, j: (l, 0, j)),
            pl.BlockSpec((1, 1, tn), lambda l, j: (l, 0, j)),
        ],
        out_specs=pl.BlockSpec((1, SUBLANES, tn), lambda l, j: (l, 0, j)),
        out_shape=jax.ShapeDtypeStruct((depth, SUBLANES, n), F32),
        compiler_params=_params("arbitrary", "arbitrary"),
        name="mod_vectors",
    )(cond, w_mod, b_mod.reshape(depth, 1, n))


def _ffn_kernel(x_ref, mod_ref, wg_ref, wu_ref, wo_ref, lng_ref, lnb_ref, *rest, k0, alpha, emit_h):
    if emit_h:
        o_ref, h2_ref, h_scr, acc = rest
    else:
        o_ref, h_scr, acc = rest
    f = pl.program_id(1)

    @pl.when(f == 0)
    def _():
        m = mod_ref[0]
        h_scr[...] = (x_ref[...] * (1.0 + m[k0 + 1:k0 + 2]) + m[k0:k0 + 1]).astype(BF16)
        acc[...] = jnp.zeros_like(acc)

    h = h_scr[...]
    g = jnp.dot(h, wg_ref[...], preferred_element_type=F32)
    u = jnp.dot(h, wu_ref[...], preferred_element_type=F32)
    a = (g * _sigmoid(g) * u).astype(BF16)
    acc[...] += jnp.dot(a, wo_ref[...], preferred_element_type=F32)

    @pl.when(f == pl.num_programs(1) - 1)
    def _():
        m = mod_ref[0]
        z = alpha * x_ref[...] + 0.5 * m[k0 + 2:k0 + 3] * acc[...]
        o = _layer_norm(z, lng_ref[...], lnb_ref[...])
        o_ref[...] = o
        if emit_h:
            h2_ref[...] = (o * (1.0 + m[k0 + 4:k0 + 5]) + m[k0 + 3:k0 + 4]).astype(BF16)


def _ffn_sublayer(x, mods, mod_row, w_in, w_out, ln_g, ln_b, *, k0, alpha, emit_h):
    t, d = x.shape
    f_dim = w_out.shape[0]
    tm, tf = ROW_TILE, 512
    nf = f_dim // tf
    out_shape = [jax.ShapeDtypeStruct((t, d), F32)]
    out_specs = [pl.BlockSpec((tm, d), lambda i, f: (i, 0))]
    if emit_h:
        out_shape.append(jax.ShapeDtypeStruct((t, d), BF16))
        out_specs.append(pl.BlockSpec((tm, d), lambda i, f: (i, 0)))
    res = pl.pallas_call(
        functools.partial(_ffn_kernel, k0=k0, alpha=alpha, emit_h=emit_h),
        grid=(t // tm, nf),
        in_specs=[
            pl.BlockSpec((tm, d), lambda i, f: (i, 0)),
            pl.BlockSpec((1, N_MOD, d), lambda i, f: (mod_row(i), 0, 0)),
            pl.BlockSpec((d, tf), lambda i, f: (0, f)),
            pl.BlockSpec((d, tf), lambda i, f: (0, f + nf)),
            pl.BlockSpec((tf, d), lambda i, f: (f, 0)),
            pl.BlockSpec((1, d), lambda i, f: (0, 0)),
            pl.BlockSpec((1, d), lambda i, f: (0, 0)),
        ],
        out_specs=out_specs,
        out_shape=out_shape,
        scratch_shapes=[pltpu.VMEM((tm, d), BF16), pltpu.VMEM((tm, d), F32)],
        compiler_params=_params("arbitrary", "arbitrary"),
        name="ffn_sublayer",
    )(x, mods, w_in, w_in, w_out, ln_g.reshape(1, d), ln_b.reshape(1, d))
    return res if emit_h else res[0]


def _matmul_kernel(a_ref, b_ref, o_ref):
    o_ref[...] = jnp.dot(a_ref[...], b_ref[...], preferred_element_type=F32).astype(o_ref.dtype)


def _project(h, w, n_cols):
    t, d = h.shape
    tm = min(t, 1024)
    tn = 1024
    return pl.pallas_call(
        _matmul_kernel,
        grid=(t // tm, n_cols // tn),
        in_specs=[
            pl.BlockSpec((tm, d), lambda i, j: (i, 0)),
            pl.BlockSpec((d, tn), lambda i, j: (0, j)),
        ],
        out_specs=pl.BlockSpec((tm, tn), lambda i, j: (i, j)),
        out_shape=jax.ShapeDtypeStruct((t, n_cols), BF16),
        compiler_params=_params("arbitrary", "arbitrary"),
        name="in_proj",
    )(h, w)


def _fill_halo(scr, x3, lo, hi, seq_chunks):
    n = x3.shape[0]
    scr[lo:lo + n] = x3
    tile = x3.shape[1:]
    if seq_chunks is None:
        if lo:
            scr[0:lo] = jnp.zeros((lo,) + tile, F32)
        if hi:
            scr[lo + n:lo + n + hi] = jnp.zeros((hi,) + tile, F32)
        return
    chunk = lax.broadcasted_iota(jnp.int32, tile, 0) % seq_chunks
    for t in range(lo):
        prev = pltpu.roll(x3[n - lo + t], 1, 0)
        scr[t] = jnp.where(chunk != 0, prev, 0.0)
    for t in range(hi):
        nxt = pltpu.roll(x3[t], SUBLANES - 1, 0)
        scr[lo + n + t] = jnp.where(chunk != seq_chunks - 1, nxt, 0.0)


def _conv_from_halo(scr, w, n):
    out = w[0:1] * scr[0:n]
    for k in range(1, w.shape[0]):
        out = out + w[k:k + 1] * scr[k:k + n]
    return out


def _rglru_kernel(p_ref, cw_ref, cb_ref, gw_ref, gb_ref, lam_ref, h0f_ref, h0b_ref, *rest,
                  chunk_len, seq_chunks, wrap, emit_state):
    if emit_state:
        h_ref, sf_ref, sb_ref, x_scr, a_f, v_f, a_b, v_b = rest
    else:
        h_ref, x_scr, a_f, v_f, a_b, v_b = rest
    n_seg = chunk_len // SEG
    cw = cw_ref[...]
    cb = cb_ref[...]
    lam = lam_ref[...]
    log_sig = jnp.minimum(lam, 0.0) - jnp.log1p(jnp.exp(-jnp.abs(lam)))
    cfac = RG_C * log_sig
    gw = gw_ref[0]
    gb = gb_ref[0]
    a_scr = (a_f, a_b)
    v_scr = (v_f, v_b)

    def gates(si, _):
        r0 = pl.multiple_of(si * ROW_TILE, ROW_TILE)
        x3 = p_ref[pl.ds(r0, ROW_TILE), :].astype(F32).reshape(SEG, SUBLANES, LANES)
        _fill_halo(x_scr, x3, RG_PAD_LO, RG_CONV - 1 - RG_PAD_LO, seq_chunks if wrap else None)
        xc = (_conv_from_halo(x_scr, cw, SEG) + cb).reshape(ROW_TILE, LANES)
        lg = jnp.dot(xc.astype(BF16), gw, preferred_element_type=F32) + gb
        j0 = pl.multiple_of(si * SEG, SEG)
        for d in range(2):
            r = _sigmoid(lg[:, (2 * d) * LANES:(2 * d + 1) * LANES])
            i = _sigmoid(lg[:, (2 * d + 1) * LANES:(2 * d + 2) * LANES])
            log_a = cfac[d:d + 1] * r
            a = jnp.exp(log_a)
            v = jnp.sqrt(-jnp.tanh(log_a) * (1.0 + a * a)) * (i * xc)
            a_scr[d][pl.ds(j0, SEG)] = a.reshape(SEG, SUBLANES, LANES)
            v_scr[d][pl.ds(j0, SEG)] = v.reshape(SEG, SUBLANES, LANES)
        return 0

    lax.fori_loop(0, n_seg, gates, 0)

    def step(j, carry):
        hf, pf, hb, pb = carry
        jb = chunk_len - 1 - j
        a = a_f[j]
        hf = a * hf + v_f[j]
        pf = a * pf
        v_f[j] = hf
        a_f[j] = pf
        a = a_b[jb]
        hb = a * hb + v_b[jb]
        pb = a * pb
        v_b[jb] = hb
        a_b[jb] = pb
        return hf, pf, hb, pb

    zero = jnp.zeros((SUBLANES, LANES), F32)
    one = jnp.ones((SUBLANES, LANES), F32)
    hf, pf, hb, pb = lax.fori_loop(0, chunk_len, step, (zero, one, zero, one), unroll=8)

    sub = lax.broadcasted_iota(jnp.int32, (SUBLANES, LANES), 0)
    h0f = h0f_ref[0]
    h0b = h0b_ref[0]
    cf = jnp.zeros((SUBLANES, LANES), F32)
    row = None
    for s in range(SUBLANES):
        if s % seq_chunks == 0:
            row = h0f[s:s + 1]
        else:
            row = hf[s - 1:s] + pf[s - 1:s] * row
        cf = jnp.where(sub == s, row, cf)
    cbk = jnp.zeros((SUBLANES, LANES), F32)
    for s in reversed(range(SUBLANES)):
        if s % seq_chunks == seq_chunks - 1:
            row = h0b[s:s + 1]
        else:
            row = hb[s + 1:s + 2] + pb[s + 1:s + 2] * row
        cbk = jnp.where(sub == s, row, cbk)

    if emit_state:
        sf_ref[0] = hf + pf * cf
        sb_ref[0] = hb + pb * cbk

    def combine(si, _):
        j0 = pl.multiple_of(si * SEG, SEG)
        r0 = pl.multiple_of(si * ROW_TILE, ROW_TILE)
        h = (v_f[pl.ds(j0, SEG)] + a_f[pl.ds(j0, SEG)] * cf) + (v_b[pl.ds(j0, SEG)] + a_b[pl.ds(j0, SEG)] * cbk)
        h_ref[pl.ds(r0, ROW_TILE), :] = h.reshape(ROW_TILE, LANES).astype(h_ref.dtype)
        return 0

    lax.fori_loop(0, n_seg, combine, 0)


def _rglru_branch(p, conv_w, conv_b, gate_w, gate_b, lam, h0f, h0b, *, chunk_len, seq_chunks, wrap, emit_state):
    t = p.shape[0]
    w = conv_w.shape[1]
    rows = chunk_len * SUBLANES
    groups = t // rows
    cb = LANES
    out_shape = [jax.ShapeDtypeStruct((t, w), BF16)]
    out_specs = [pl.BlockSpec((rows, cb), lambda g, c: (g, c))]
    if emit_state:
        for _ in range(2):
            out_shape.append(jax.ShapeDtypeStruct((groups, SUBLANES, w), F32))
            out_specs.append(pl.BlockSpec((1, SUBLANES, cb), lambda g, c: (g, 0, c)))
    return pl.pallas_call(
        functools.partial(_rglru_kernel, chunk_len=chunk_len, seq_chunks=seq_chunks, wrap=wrap,
                          emit_state=emit_state),
        grid=(groups, w // cb),
        in_specs=[
            pl.BlockSpec((rows, cb), lambda g, c: (g, c)),
            pl.BlockSpec((RG_CONV, cb), lambda g, c: (0, c)),
            pl.BlockSpec((1, cb), lambda g, c: (0, c)),
            pl.BlockSpec((1, cb, 4 * cb), lambda g, c: (c, 0, 0)),
            pl.BlockSpec((1, 1, 4 * cb), lambda g, c: (c, 0, 0)),
            pl.BlockSpec((2, cb), lambda g, c: (0, c)),
            pl.BlockSpec((1, SUBLANES, cb), lambda g, c: (g, 0, c)),
            pl.BlockSpec((1, SUBLANES, cb), lambda g, c: (g, 0, c)),
        ],
        out_specs=out_specs,
        out_shape=out_shape,
        scratch_shapes=[pltpu.VMEM((SEG + RG_CONV - 1, SUBLANES, cb), F32)]
        + [pltpu.VMEM((chunk_len, SUBLANES, cb), F32) for _ in range(4)],
        compiler_params=_params("arbitrary", "arbitrary"),
        name="rglru_branch",
    )(p, conv_w, conv_b.reshape(1, w), gate_w, gate_b, lam, h0f, h0b)


def _merge_kernel(hrg_ref, gate_ref, scb_ref, scc_ref, scx_ref, grg_ref, gsc_ref, scw_ref, bm_ref,
                  wrg_ref, wsc_ref, o_ref, a_scr, s_scr, q_scr, *, seq_chunks, wrap, tc):
    n = pl.program_id(1)
    d = a_scr.shape[1]

    @pl.when(n == 0)
    def _():
        for c0 in range(0, d, tc):
            cs = slice(c0, c0 + tc)
            hr = hrg_ref[:, cs].astype(F32)
            a_scr[:, cs] = (hr * _gelu_tanh(gate_ref[:, cs].astype(F32))).astype(BF16)
            q = scc_ref[:, cs].astype(F32) * scx_ref[:, cs].astype(F32)
            _fill_halo(q_scr, q.reshape(SEG, SUBLANES, tc), SC_PAD_LO, SC_CONV - 1 - SC_PAD_LO,
                       seq_chunks if wrap else None)
            cq = _conv_from_halo(q_scr, scw_ref[:, cs], SEG).reshape(ROW_TILE, tc)
            s_scr[:, cs] = (scb_ref[:, cs].astype(F32) * cq).astype(BF16)

    y_rg = jnp.dot(a_scr[...], wrg_ref[...], preferred_element_type=F32)
    y_sc = jnp.dot(s_scr[...], wsc_ref[...], preferred_element_type=F32)
    bm = bm_ref[...]
    m = (_sigmoid(grg_ref[...].astype(F32) + bm[0:1]) * y_rg
         + _sigmoid(gsc_ref[...].astype(F32) + bm[1:2]) * y_sc)
    o_ref[...] = m.astype(o_ref.dtype)


def _mixer_merge(p, h_rg, sc_conv_w, b_merge, w_rg_out, w_sc_out, *, seq_chunks, wrap):
    t, d = h_rg.shape
    tm, tn, tc = ROW_TILE, 512, 512
    nb = d // tn
    sec = lambda k: pl.BlockSpec((tm, d), lambda i, n: (i, k))
    return pl.pallas_call(
        functools.partial(_merge_kernel, seq_chunks=seq_chunks, wrap=wrap, tc=tc),
        grid=(t // tm, nb),
        in_specs=[
            pl.BlockSpec((tm, d), lambda i, n: (i, 0)),
            sec(1), sec(2), sec(3), sec(4),
            pl.BlockSpec((tm, tn), lambda i, n: (i, 5 * nb + n)),
            pl.BlockSpec((tm, tn), lambda i, n: (i, 6 * nb + n)),
            pl.BlockSpec((SC_CONV, d), lambda i, n: (0, 0)),
            pl.BlockSpec((2, tn), lambda i, n: (0, n)),
            pl.BlockSpec((d, tn), lambda i, n: (0, n)),
            pl.BlockSpec((d, tn), lambda i, n: (0, n)),
        ],
        out_specs=pl.BlockSpec((tm, tn), lambda i, n: (i, n)),
        out_shape=jax.ShapeDtypeStruct((t, d), BF16),
        scratch_shapes=[pltpu.VMEM((tm, d), BF16), pltpu.VMEM((tm, d), BF16),
                        pltpu.VMEM((SEG + SC_CONV - 1, SUBLANES, tc), F32)],
        compiler_params=_params("arbitrary", "arbitrary"),
        name="mixer_merge",
    )(h_rg, p, p, p, p, p, p, sc_conv_w, b_merge, w_rg_out, w_sc_out)


def _oproj_kernel(m_ref, x_ref, mod_ref, wo_ref, lng_ref, lnb_ref, o_ref, *, alpha):
    y = jnp.dot(m_ref[...], wo_ref[...], preferred_element_type=F32)
    z = alpha * x_ref[...] + mod_ref[0][5:6] * y
    o_ref[...] = _layer_norm(z, lng_ref[...], lnb_ref[...])


def _mixer_oproj(m, x, mods, mod_row, w_o, ln_g, ln_b, *, alpha):
    t, d = x.shape
    tm = ROW_TILE
    return pl.pallas_call(
        functools.partial(_oproj_kernel, alpha=alpha),
        grid=(t // tm,),
        in_specs=[
            pl.BlockSpec((tm, d), lambda i: (i, 0)),
            pl.BlockSpec((tm, d), lambda i: (i, 0)),
            pl.BlockSpec((1, N_MOD, d), lambda i: (mod_row(i), 0, 0)),
            pl.BlockSpec((d, d), lambda i: (0, 0)),
            pl.BlockSpec((1, d), lambda i: (0, 0)),
            pl.BlockSpec((1, d), lambda i: (0, 0)),
        ],
        out_specs=pl.BlockSpec((tm, d), lambda i: (i, 0)),
        out_shape=jax.ShapeDtypeStruct((t, d), F32),
        compiler_params=_params("arbitrary"),
        name="mixer_oproj",
    )(m, x, mods, w_o, ln_g.reshape(1, d), ln_b.reshape(1, d))


def kernel(x, c, ctx, c_ctx, w_mod, b_mod, ln_g, ln_b, ffn1_w_in, ffn1_w_out, ffn2_w_in, ffn2_w_out, w_in, rg_conv_w, rg_conv_b, rg_gate_w, rg_gate_b, rg_lam, sc_conv_w, w_rg_out, w_sc_out, b_merge, w_o):
    batch, seq, d = x.shape
    ctx_len = ctx.shape[1]
    depth = w_mod.shape[0]
    w_rg = rg_conv_w.shape[2]
    alpha = (2 * depth) ** 0.25
    lat_chunk = seq // LAT_CHUNKS
    ctx_chunk = ctx_len // CTX_CHUNKS
    ctx_group = SUBLANES // CTX_CHUNKS
    assert lat_chunk % GRID_W == 0 and ctx_chunk == SEG and batch % ctx_group == 0
    assert w_rg == RG_BLOCKS * LANES and d % LANES == 0

    xl = x.reshape(batch, LAT_CHUNKS, lat_chunk, d).transpose(0, 2, 1, 3).reshape(batch * seq, d)
    xc = ctx.reshape(batch // ctx_group, ctx_group, CTX_CHUNKS, ctx_chunk, d).transpose(0, 3, 1, 2, 4)
    xc = xc.reshape(batch * ctx_len, d)

    cond = jnp.zeros((SUBLANES, d), F32).at[:batch].set(c).at[batch].set(c_ctx)
    mods = _mod_vectors(cond, w_mod, b_mod).reshape(depth, SUBLANES, N_MOD, d)
    lat_tiles = seq // ROW_TILE
    lat_row = lambda i: i // lat_tiles
    ctx_row = lambda i: batch

    bf = lambda a: a.astype(BF16)
    zeros_state = jnp.zeros((batch // ctx_group, SUBLANES, w_rg), F32)

    for l in range(depth):
        last = l == depth - 1
        m_l = mods[l]
        f1_in, f1_out, f2_in, f2_out = bf(ffn1_w_in[l]), bf(ffn1_w_out[l]), bf(ffn2_w_in[l]), bf(ffn2_w_out[l])
        w_in_l = bf(w_in[l])
        gw = bf(rg_gate_w[l].transpose(2, 3, 0, 1, 4).reshape(RG_BLOCKS, LANES, 4 * LANES))
        gb = rg_gate_b[l].reshape(4, RG_BLOCKS, LANES).transpose(1, 0, 2).reshape(RG_BLOCKS, 1, 4 * LANES)
        rg_args = (rg_conv_w[l], rg_conv_b[l], gw, gb, rg_lam[l])
        w_rg_out_l, w_sc_out_l, w_o_l = bf(w_rg_out[l]), bf(w_sc_out[l]), bf(w_o[l])

        xl, hl = _ffn_sublayer(xl, m_l, lat_row, f1_in, f1_out, ln_g[l, 0], ln_b[l, 0], k0=0, alpha=alpha, emit_h=True)
        xc, hc = _ffn_sublayer(xc, m_l, ctx_row, f1_in, f1_out, ln_g[l, 0], ln_b[l, 0], k0=0, alpha=alpha, emit_h=True)

        pc = _project(hc, w_in_l, w_rg if last else N_PROJ * d)
        hrg_c, s_f, s_b = _rglru_branch(pc, *rg_args, zeros_state, zeros_state, chunk_len=ctx_chunk,
                                        seq_chunks=CTX_CHUNKS, wrap=True, emit_state=True)
        if not last:
            mc = _mixer_merge(pc, hrg_c, sc_conv_w[l], b_merge[l], w_rg_out_l, w_sc_out_l,
                              seq_chunks=CTX_CHUNKS, wrap=True)
            xc = _mixer_oproj(mc, xc, m_l, ctx_row, w_o_l, ln_g[l, 1], ln_b[l, 1], alpha=alpha)
        s_f = s_f.reshape(batch, CTX_CHUNKS, w_rg)[:, CTX_CHUNKS - 1]
        s_b = s_b.reshape(batch, CTX_CHUNKS, w_rg)[:, 0]
        h0f = jnp.broadcast_to(s_f[:, None, :], (batch, SUBLANES, w_rg))
        h0b = jnp.broadcast_to(s_b[:, None, :], (batch, SUBLANES, w_rg))

        pl_ = _project(hl, w_in_l, N_PROJ * d)
        (hrg_l,) = _rglru_branch(pl_, *rg_args, h0f, h0b, chunk_len=lat_chunk, seq_chunks=LAT_CHUNKS,
                                 wrap=False, emit_state=False)
        ml = _mixer_merge(pl_, hrg_l, sc_conv_w[l], b_merge[l], w_rg_out_l, w_sc_out_l,
                          seq_chunks=LAT_CHUNKS, wrap=False)
        xl = _mixer_oproj(ml, xl, m_l, lat_row, w_o_l, ln_g[l, 1], ln_b[l, 1], alpha=alpha)

        xl = _ffn_sublayer(xl, m_l, lat_row, f2_in, f2_out, ln_g[l, 2], ln_b[l, 2], k0=6, alpha=alpha, emit_h=False)
        if not last:
            xc = _ffn_sublayer(xc, m_l, ctx_row, f2_in, f2_out, ln_g[l, 2], ln_b[l, 2], k0=6, alpha=alpha, emit_h=False)

    return xl.reshape(batch, lat_chunk, LAT_CHUNKS, d).transpose(0, 2, 1, 3).reshape(batch, seq, d)
```

```python
import functools
import math

import jax
import jax.numpy as jnp
from jax import lax
from jax.experimental import pallas as pl
from jax.experimental.pallas import tpu as pltpu

F32 = jnp.float32
BF16 = jnp.bfloat16

SUBLANES = 8
LANES = 128
VMEM_BYTES_V7X = 64 * 1024 * 1024
VMEM_LIMIT = VMEM_BYTES_V7X - 8 * 1024 * 1024

GRID_W = 64
N_MOD = 9
RG_BLOCKS = 16
RG_CONV = 4
RG_PAD_LO = 2
SC_CONV = 3
SC_PAD_LO = 1
RG_C = 8.0
LN_EPS = 1e-5
N_PROJ = 7

SEG = GRID_W
ROW_TILE = SEG * SUBLANES
LAT_CHUNKS = SUBLANES
CTX_CHUNKS = 4
SCAN_BLOCK = 8
LOG2_E = 1.0 / math.log(2.0)


def _sigmoid(x):
    return 1.0 / (1.0 + jnp.exp(-x))


def _sigmoid_tanh(x):
    return 0.5 + 0.5 * jnp.tanh(0.5 * x)


def _gelu_tanh(x):
    return x * (0.5 * (1.0 + jnp.tanh(math.sqrt(2.0 / math.pi) * (x + 0.044715 * (x * x * x)))))


def _layer_norm(z, g, b):
    mu = jnp.mean(z, axis=-1, keepdims=True)
    zc = z - mu
    var = jnp.mean(zc * zc, axis=-1, keepdims=True)
    return zc * lax.rsqrt(var + LN_EPS) * g + b


def _params(*sem):
    return pltpu.CompilerParams(dimension_semantics=sem, vmem_limit_bytes=VMEM_LIMIT)


def _mod_kernel(c_ref, w_ref, b_ref, o_ref):
    @pl.when(pl.program_id(1) == 0)
    def _():
        o_ref[...] = jnp.broadcast_to(b_ref[...], o_ref.shape)

    c = c_ref[...]
    s = (c * _sigmoid(c)).astype(BF16)
    o_ref[...] += jnp.dot(s, w_ref[...].astype(BF16), preferred_element_type=F32)


def _mod_vectors(cond, w_mod, b_mod):
    depth, d, n = w_mod.shape
    tk = LANES
    return pl.pallas_call(
        _mod_kernel,
        grid=(depth, d // tk),
        in_specs=[
            pl.BlockSpec((SUBLANES, tk), lambda l, k: (0, k)),
            pl.BlockSpec((None, tk, n), lambda l, k: (l, k, 0)),
            pl.BlockSpec((None, 1, n), lambda l, k: (l, 0, 0)),
        ],
        out_specs=pl.BlockSpec((None, SUBLANES, n), lambda l, k: (l, 0, 0)),
        out_shape=jax.ShapeDtypeStruct((depth, SUBLANES, n), F32),
        compiler_params=_params("arbitrary", "arbitrary"),
        name="mod_vectors",
    )(cond, w_mod, b_mod.reshape(depth, 1, n))


def _ffn_kernel(x_ref, mod_ref, wg_ref, wu_ref, wo_ref, lng_ref, lnb_ref, *rest, k0, alpha, emit_h):
    if emit_h:
        o_ref, h2_ref, h_scr, acc = rest
    else:
        o_ref, h_scr, acc = rest
    f = pl.program_id(1)

    @pl.when(f == 0)
    def _():
        m = mod_ref[...]
        h_scr[...] = (x_ref[...] * (1.0 + m[k0 + 1:k0 + 2]) + m[k0:k0 + 1]).astype(BF16)
        acc[...] = jnp.zeros_like(acc)

    h = h_scr[...]
    g = jnp.dot(h, wg_ref[...], preferred_element_type=F32)
    u = jnp.dot(h, wu_ref[...], preferred_element_type=F32)
    a = (g * _sigmoid(g) * u).astype(BF16)
    acc[...] += jnp.dot(a, wo_ref[...], preferred_element_type=F32)

    @pl.when(f == pl.num_programs(1) - 1)
    def _():
        m = mod_ref[...]
        z = alpha * x_ref[...] + 0.5 * m[k0 + 2:k0 + 3] * acc[...]
        o = _layer_norm(z, lng_ref[...], lnb_ref[...])
        o_ref[...] = o
        if emit_h:
            h2_ref[...] = (o * (1.0 + m[k0 + 4:k0 + 5]) + m[k0 + 3:k0 + 4]).astype(BF16)


def _ffn_sublayer(x, mods, mod_row, w_in, w_out, ln_g, ln_b, *, layer, ln_idx, k0, alpha, emit_h):
    t, d = x.shape
    f_dim = w_out.shape[1]
    tm, tf = ROW_TILE, 512
    nf = f_dim // tf
    out_shape = [jax.ShapeDtypeStruct((t, d), F32)]
    out_specs = [pl.BlockSpec((tm, d), lambda i, f: (i, 0))]
    if emit_h:
        out_shape.append(jax.ShapeDtypeStruct((t, d), BF16))
        out_specs.append(pl.BlockSpec((tm, d), lambda i, f: (i, 0)))
    res = pl.pallas_call(
        functools.partial(_ffn_kernel, k0=k0, alpha=alpha, emit_h=emit_h),
        grid=(t // tm, nf),
        in_specs=[
            pl.BlockSpec((tm, d), lambda i, f: (i, 0)),
            pl.BlockSpec((None, None, N_MOD, d), lambda i, f: (layer, mod_row(i), 0, 0)),
            pl.BlockSpec((None, d, tf), lambda i, f: (layer, 0, f)),
            pl.BlockSpec((None, d, tf), lambda i, f: (layer, 0, f + nf)),
            pl.BlockSpec((None, tf, d), lambda i, f: (layer, f, 0)),
            pl.BlockSpec((None, 1, d), lambda i, f: (ln_idx, 0, 0)),
            pl.BlockSpec((None, 1, d), lambda i, f: (ln_idx, 0, 0)),
        ],
        out_specs=out_specs,
        out_shape=out_shape,
        scratch_shapes=[pltpu.VMEM((tm, d), BF16), pltpu.VMEM((tm, d), F32)],
        compiler_params=_params("arbitrary", "arbitrary"),
        name="ffn_sublayer",
    )(x, mods, w_in, w_in, w_out, ln_g, ln_b)
    return res if emit_h else res[0]


def _matmul_kernel(a_ref, b_ref, o_ref):
    o_ref[...] = jnp.dot(a_ref[...], b_ref[...], preferred_element_type=F32).astype(o_ref.dtype)


def _project(h, w, n_cols, *, layer):
    t, d = h.shape
    tm = min(t, 1024)
    tn = 1024
    return pl.pallas_call(
        _matmul_kernel,
        grid=(t // tm, n_cols // tn),
        in_specs=[
            pl.BlockSpec((tm, d), lambda i, j: (i, 0)),
            pl.BlockSpec((None, d, tn), lambda i, j: (layer, 0, j)),
        ],
        out_specs=pl.BlockSpec((tm, tn), lambda i, j: (i, j)),
        out_shape=jax.ShapeDtypeStruct((t, n_cols), BF16),
        compiler_params=_params("arbitrary", "arbitrary"),
        name="in_proj",
    )(h, w)


def _fill_halo(scr, x3, lo, hi, seq_chunks):
    n = x3.shape[0]
    scr[lo:lo + n] = x3
    tile = x3.shape[1:]
    if seq_chunks is None:
        if lo:
            scr[0:lo] = jnp.zeros((lo,) + tile, F32)
        if hi:
            scr[lo + n:lo + n + hi] = jnp.zeros((hi,) + tile, F32)
        return
    chunk = lax.broadcasted_iota(jnp.int32, tile, 0) % seq_chunks
    for t in range(lo):
        prev = pltpu.roll(x3[n - lo + t], 1, 0)
        scr[t] = jnp.where(chunk != 0, prev, 0.0)
    for t in range(hi):
        nxt = pltpu.roll(x3[t], SUBLANES - 1, 0)
        scr[lo + n + t] = jnp.where(chunk != seq_chunks - 1, nxt, 0.0)


def _conv_from_halo(scr, w, n):
    out = w[0:1] * scr[0:n]
    for k in range(1, w.shape[0]):
        out = out + w[k:k + 1] * scr[k:k + n]
    return out


def _rglru_kernel(p_ref, cw_ref, cb_ref, gw_ref, gb_ref, lam_ref, h0f_ref, h0b_ref, *rest,
                  chunk_len, seq_chunks, wrap, emit_state):
    if emit_state:
        h_ref, sf_ref, sb_ref, x_scr, a_f, v_f, a_b, v_b, h_f, p_f, h_b, p_b = rest
    else:
        h_ref, x_scr, a_f, v_f, a_b, v_b, h_f, p_f, h_b, p_b = rest
    n_seg = chunk_len // SEG
    cw = cw_ref[...]
    cb = cb_ref[...]
    lam = lam_ref[...]
    log_sig = jnp.minimum(lam, 0.0) - jnp.log1p(jnp.exp(-jnp.abs(lam)))
    e_fac = (0.5 * RG_C * LOG2_E) * log_sig
    gw_half = gw_ref[...] * 0.5
    gb_half = gb_ref[...] * 0.5
    a_scr = (a_f, a_b)
    v_scr = (v_f, v_b)

    def gates(si, _):
        r0 = pl.multiple_of(si * ROW_TILE, ROW_TILE)
        x3 = p_ref[pl.ds(r0, ROW_TILE), :].astype(F32).reshape(SEG, SUBLANES, LANES)
        _fill_halo(x_scr, x3, RG_PAD_LO, RG_CONV - 1 - RG_PAD_LO, seq_chunks if wrap else None)
        xc = (_conv_from_halo(x_scr, cw, SEG) + cb).reshape(ROW_TILE, LANES)
        t = jnp.tanh(jnp.dot(xc.astype(BF16), gw_half, preferred_element_type=F32) + gb_half)
        xh = 0.5 * xc
        j0 = pl.multiple_of(si * SEG, SEG)
        for d in range(2):
            t_r = t[:, (2 * d) * LANES:(2 * d + 1) * LANES]
            t_i = t[:, (2 * d + 1) * LANES:(2 * d + 2) * LANES]
            ef = e_fac[d:d + 1]
            a = jnp.exp2(ef * t_r + ef)
            y = 1.0 - a * a
            s = jnp.where(y > 0.0, y * lax.rsqrt(y), 0.0)
            v = s * (xh * t_i + xh)
            a_scr[d][pl.ds(j0, SEG)] = a.reshape(SEG, SUBLANES, LANES)
            v_scr[d][pl.ds(j0, SEG)] = v.reshape(SEG, SUBLANES, LANES)
        return 0

    lax.fori_loop(0, n_seg, gates, 0)

    def steps(blk, carry):
        hf, pf, hb, pb = carry
        jf = pl.ds(pl.multiple_of(blk * SCAN_BLOCK, SCAN_BLOCK), SCAN_BLOCK)
        jb = pl.ds(pl.multiple_of(chunk_len - SCAN_BLOCK - blk * SCAN_BLOCK, SCAN_BLOCK), SCAN_BLOCK)
        a, v = a_f[jf], v_f[jf]
        hs, ps = [], []
        for u in range(SCAN_BLOCK):
            hf = a[u] * hf + v[u]
            pf = a[u] * pf
            hs.append(hf)
            ps.append(pf)
        h_f[jf] = jnp.stack(hs)
        p_f[jf] = jnp.stack(ps)
        a, v = a_b[jb], v_b[jb]
        hs, ps = [], []
        for u in reversed(range(SCAN_BLOCK)):
            hb = a[u] * hb + v[u]
            pb = a[u] * pb
            hs.append(hb)
            ps.append(pb)
        h_b[jb] = jnp.stack(hs[::-1])
        p_b[jb] = jnp.stack(ps[::-1])
        return hf, pf, hb, pb

    zero = jnp.zeros((SUBLANES, LANES), F32)
    one = jnp.ones((SUBLANES, LANES), F32)
    hf, pf, hb, pb = lax.fori_loop(0, chunk_len // SCAN_BLOCK, steps, (zero, one, zero, one))

    sub = lax.broadcasted_iota(jnp.int32, (SUBLANES, LANES), 0)
    h0f = h0f_ref[...]
    h0b = h0b_ref[...]
    cf = jnp.zeros((SUBLANES, LANES), F32)
    row = None
    for s in range(SUBLANES):
        if s % seq_chunks == 0:
            row = h0f[s:s + 1]
        else:
            row = hf[s - 1:s] + pf[s - 1:s] * row
        cf = jnp.where(sub == s, row, cf)
    cbk = jnp.zeros((SUBLANES, LANES), F32)
    for s in reversed(range(SUBLANES)):
        if s % seq_chunks == seq_chunks - 1:
            row = h0b[s:s + 1]
        else:
            row = hb[s + 1:s + 2] + pb[s + 1:s + 2] * row
        cbk = jnp.where(sub == s, row, cbk)

    if emit_state:
        sf_ref[...] = hf + pf * cf
        sb_ref[...] = hb + pb * cbk

    def combine(si, _):
        j0 = pl.multiple_of(si * SEG, SEG)
        r0 = pl.multiple_of(si * ROW_TILE, ROW_TILE)
        js = pl.ds(j0, SEG)
        h = (h_f[js] + p_f[js] * cf) + (h_b[js] + p_b[js] * cbk)
        h_ref[pl.ds(r0, ROW_TILE), :] = h.reshape(ROW_TILE, LANES).astype(h_ref.dtype)
        return 0

    lax.fori_loop(0, n_seg, combine, 0)


def _rglru_branch(p, conv_w, conv_b, gate_w, gate_b, lam, h0f, h0b, *, layer, chunk_len, seq_chunks, wrap,
                  emit_state):
    t = p.shape[0]
    w = conv_w.shape[2]
    rows = chunk_len * SUBLANES
    groups = t // rows
    cb = LANES
    out_shape = [jax.ShapeDtypeStruct((t, w), BF16)]
    out_specs = [pl.BlockSpec((rows, cb), lambda g, c: (g, c))]
    if emit_state:
        for _ in range(2):
            out_shape.append(jax.ShapeDtypeStruct((groups, SUBLANES, w), F32))
            out_specs.append(pl.BlockSpec((None, SUBLANES, cb), lambda g, c: (g, 0, c)))
    return pl.pallas_call(
        functools.partial(_rglru_kernel, chunk_len=chunk_len, seq_chunks=seq_chunks, wrap=wrap,
                          emit_state=emit_state),
        grid=(groups, w // cb),
        in_specs=[
            pl.BlockSpec((rows, cb), lambda g, c: (g, c)),
            pl.BlockSpec((None, RG_CONV, cb), lambda g, c: (layer, 0, c)),
            pl.BlockSpec((None, 1, cb), lambda g, c: (layer, 0, c)),
            pl.BlockSpec((None, None, cb, 4 * cb), lambda g, c: (layer, c, 0, 0)),
            pl.BlockSpec((None, None, 1, 4 * cb), lambda g, c: (layer, c, 0, 0)),
            pl.BlockSpec((None, 2, cb), lambda g, c: (layer, 0, c)),
            pl.BlockSpec((None, SUBLANES, cb), lambda g, c: (g, 0, c)),
            pl.BlockSpec((None, SUBLANES, cb), lambda g, c: (g, 0, c)),
        ],
        out_specs=out_specs,
        out_shape=out_shape,
        scratch_shapes=[pltpu.VMEM((SEG + RG_CONV - 1, SUBLANES, cb), F32)]
        + [pltpu.VMEM((chunk_len, SUBLANES, cb), F32) for _ in range(8)],
        compiler_params=_params("arbitrary", "arbitrary"),
        name="rglru_branch",
    )(p, conv_w, conv_b, gate_w, gate_b, lam, h0f, h0b)


def _merge_kernel(hrg_ref, gate_ref, scb_ref, scc_ref, scx_ref, grg_ref, gsc_ref, scw_ref, bm_ref,
                  wrg_ref, wsc_ref, o_ref, a_scr, s_scr, q_scr, acc_rg, acc_sc, *, seq_chunks, wrap):
    k = pl.program_id(1)
    nk = pl.num_programs(1) - 1
    tc = a_scr.shape[2]

    def produce(slot):
        a_scr[slot] = (hrg_ref[...].astype(F32) * _gelu_tanh(gate_ref[...].astype(F32))).astype(BF16)
        q = scc_ref[...].astype(F32) * scx_ref[...].astype(F32)
        _fill_halo(q_scr, q.reshape(SEG, SUBLANES, tc), SC_PAD_LO, SC_CONV - 1 - SC_PAD_LO,
                   seq_chunks if wrap else None)
        cq = _conv_from_halo(q_scr, scw_ref[...], SEG).reshape(ROW_TILE, tc)
        s_scr[slot] = (scb_ref[...].astype(F32) * cq).astype(BF16)

    def consume(slot):
        acc_rg[...] += jnp.dot(a_scr[slot], wrg_ref[...], preferred_element_type=F32)
        acc_sc[...] += jnp.dot(s_scr[slot], wsc_ref[...], preferred_element_type=F32)

    @pl.when(k == 0)
    def _():
        acc_rg[...] = jnp.zeros_like(acc_rg)
        acc_sc[...] = jnp.zeros_like(acc_sc)
        produce(0)

    @pl.when(jnp.logical_and(k > 0, k < nk))
    def _():
        produce(k % 2)
        consume((k + 1) % 2)

    @pl.when(k == nk)
    def _():
        consume((k + 1) % 2)
        bm = bm_ref[...]
        m = (_sigmoid_tanh(grg_ref[...].astype(F32) + bm[0:1]) * acc_rg[...]
             + _sigmoid_tanh(gsc_ref[...].astype(F32) + bm[1:2]) * acc_sc[...])
        o_ref[...] = m.astype(o_ref.dtype)


def _mixer_merge(p, h_rg, sc_conv_w, b_merge, w_rg_out, w_sc_out, *, layer, seq_chunks, wrap):
    t, d = h_rg.shape
    tm, tc = ROW_TILE, 512
    nb = d // tc
    chunk = lambda sec: pl.BlockSpec((tm, tc), lambda i, k: (i, sec * nb + jnp.minimum(k, nb - 1)))
    wchunk = pl.BlockSpec((None, tc, d), lambda i, k: (layer, jnp.maximum(k - 1, 0), 0))
    return pl.pallas_call(
        functools.partial(_merge_kernel, seq_chunks=seq_chunks, wrap=wrap),
        grid=(t // tm, nb + 1),
        in_specs=[
            chunk(0), chunk(1), chunk(2), chunk(3), chunk(4),
            pl.BlockSpec((tm, d), lambda i, k: (i, 5)),
            pl.BlockSpec((tm, d), lambda i, k: (i, 6)),
            pl.BlockSpec((None, SC_CONV, tc), lambda i, k: (layer, 0, jnp.minimum(k, nb - 1))),
            pl.BlockSpec((None, 2, d), lambda i, k: (layer, 0, 0)),
            wchunk, wchunk,
        ],
        out_specs=pl.BlockSpec((tm, d), lambda i, k: (i, 0)),
        out_shape=jax.ShapeDtypeStruct((t, d), BF16),
        scratch_shapes=[pltpu.VMEM((2, tm, tc), BF16), pltpu.VMEM((2, tm, tc), BF16),
                        pltpu.VMEM((SEG + SC_CONV - 1, SUBLANES, tc), F32),
                        pltpu.VMEM((tm, d), F32), pltpu.VMEM((tm, d), F32)],
        compiler_params=_params("arbitrary", "arbitrary"),
        name="mixer_merge",
    )(h_rg, p, p, p, p, p, p, sc_conv_w, b_merge, w_rg_out, w_sc_out)


def _oproj_kernel(m_ref, x_ref, mod_ref, wo_ref, lng_ref, lnb_ref, o_ref, *, alpha):
    y = jnp.dot(m_ref[...], wo_ref[...], preferred_element_type=F32)
    z = alpha * x_ref[...] + mod_ref[...][5:6] * y
    o_ref[...] = _layer_norm(z, lng_ref[...], lnb_ref[...])


def _mixer_oproj(m, x, mods, mod_row, w_o, ln_g, ln_b, *, layer, ln_idx, alpha):
    t, d = x.shape
    tm = ROW_TILE
    return pl.pallas_call(
        functools.partial(_oproj_kernel, alpha=alpha),
        grid=(t // tm,),
        in_specs=[
            pl.BlockSpec((tm, d), lambda i: (i, 0)),
            pl.BlockSpec((tm, d), lambda i: (i, 0)),
            pl.BlockSpec((None, None, N_MOD, d), lambda i: (layer, mod_row(i), 0, 0)),
            pl.BlockSpec((None, d, d), lambda i: (layer, 0, 0)),
            pl.BlockSpec((None, 1, d), lambda i: (ln_idx, 0, 0)),
            pl.BlockSpec((None, 1, d), lambda i: (ln_idx, 0, 0)),
        ],
        out_specs=pl.BlockSpec((tm, d), lambda i: (i, 0)),
        out_shape=jax.ShapeDtypeStruct((t, d), F32),
        compiler_params=_params("arbitrary"),
        name="mixer_oproj",
    )(m, x, mods, w_o, ln_g, ln_b)


def kernel(x, c, ctx, c_ctx, w_mod, b_mod, ln_g, ln_b, ffn1_w_in, ffn1_w_out, ffn2_w_in, ffn2_w_out, w_in, rg_conv_w, rg_conv_b, rg_gate_w, rg_gate_b, rg_lam, sc_conv_w, w_rg_out, w_sc_out, b_merge, w_o):
    batch, seq, d = x.shape
    ctx_len = ctx.shape[1]
    depth = w_mod.shape[0]
    w_rg = rg_conv_w.shape[2]
    n_ln = ln_g.shape[1]
    alpha = (2 * depth) ** 0.25
    lat_chunk = seq // LAT_CHUNKS
    ctx_chunk = ctx_len // CTX_CHUNKS
    ctx_group = SUBLANES // CTX_CHUNKS
    assert lat_chunk % GRID_W == 0 and ctx_chunk == SEG and batch % ctx_group == 0
    assert w_rg == RG_BLOCKS * LANES and d % LANES == 0

    xl = x.reshape(batch, LAT_CHUNKS, lat_chunk, d).transpose(0, 2, 1, 3).reshape(batch * seq, d)
    xc = ctx.reshape(batch // ctx_group, ctx_group, CTX_CHUNKS, ctx_chunk, d).transpose(0, 3, 1, 2, 4)
    xc = xc.reshape(batch * ctx_len, d)

    cond = jnp.zeros((SUBLANES, d), F32).at[:batch].set(c).at[batch].set(c_ctx)
    mods = _mod_vectors(cond, w_mod, b_mod).reshape(depth, SUBLANES, N_MOD, d)
    lat_tiles = seq // ROW_TILE
    lat_row = lambda i: i // lat_tiles
    ctx_row = lambda i: batch

    bf = lambda a: a.astype(BF16)
    f1_in, f1_out, f2_in, f2_out = bf(ffn1_w_in), bf(ffn1_w_out), bf(ffn2_w_in), bf(ffn2_w_out)
    w_in_b, w_rg_out_b, w_sc_out_b, w_o_b = bf(w_in), bf(w_rg_out), bf(w_sc_out), bf(w_o)
    ln_g = ln_g.reshape(depth * n_ln, 1, d)
    ln_b = ln_b.reshape(depth * n_ln, 1, d)
    gw = bf(rg_gate_w.transpose(0, 3, 4, 1, 2, 5).reshape(depth, RG_BLOCKS, LANES, 4 * LANES))
    gb = rg_gate_b.reshape(depth, 4, RG_BLOCKS, LANES).transpose(0, 2, 1, 3).reshape(depth, RG_BLOCKS, 1, 4 * LANES)
    rg_args = (rg_conv_w, rg_conv_b.reshape(depth, 1, w_rg), gw, gb, rg_lam)
    zeros_state = jnp.zeros((batch // ctx_group, SUBLANES, w_rg), F32)

    for l in range(depth):
        last = l == depth - 1
        ffn = functools.partial(_ffn_sublayer, layer=l, alpha=alpha)
        ln = lambda k: dict(ln_g=ln_g, ln_b=ln_b, ln_idx=l * n_ln + k)

        xl, hl = ffn(xl, mods, lat_row, f1_in, f1_out, **ln(0), k0=0, emit_h=True)
        xc, hc = ffn(xc, mods, ctx_row, f1_in, f1_out, **ln(0), k0=0, emit_h=True)

        pc = _project(hc, w_in_b, w_rg if last else N_PROJ * d, layer=l)
        hrg_c, s_f, s_b = _rglru_branch(pc, *rg_args, zeros_state, zeros_state, layer=l, chunk_len=ctx_chunk,
                                        seq_chunks=CTX_CHUNKS, wrap=True, emit_state=True)
        if not last:
            mc = _mixer_merge(pc, hrg_c, sc_conv_w, b_merge, w_rg_out_b, w_sc_out_b, layer=l,
                              seq_chunks=CTX_CHUNKS, wrap=True)
            xc = _mixer_oproj(mc, xc, mods, ctx_row, w_o_b, **ln(1), layer=l, alpha=alpha)
        s_f = s_f.reshape(batch, CTX_CHUNKS, w_rg)[:, CTX_CHUNKS - 1]
        s_b = s_b.reshape(batch, CTX_CHUNKS, w_rg)[:, 0]
        h0f = jnp.broadcast_to(s_f[:, None, :], (batch, SUBLANES, w_rg))
        h0b = jnp.broadcast_to(s_b[:, None, :], (batch, SUBLANES, w_rg))

        pl_ = _project(hl, w_in_b, N_PROJ * d, layer=l)
        (hrg_l,) = _rglru_branch(pl_, *rg_args, h0f, h0b, layer=l, chunk_len=lat_chunk, seq_chunks=LAT_CHUNKS,
                                 wrap=False, emit_state=False)
        ml = _mixer_merge(pl_, hrg_l, sc_conv_w, b_merge, w_rg_out_b, w_sc_out_b, layer=l,
                          seq_chunks=LAT_CHUNKS, wrap=False)
        xl = _mixer_oproj(ml, xl, mods, lat_row, w_o_b, **ln(1), layer=l, alpha=alpha)

        xl = ffn(xl, mods, lat_row, f2_in, f2_out, **ln(2), k0=6, emit_h=False)
        if not last:
            xc = ffn(xc, mods, ctx_row, f2_in, f2_out, **ln(2), k0=6, emit_h=False)

    return xl.reshape(batch, lat_chunk, LAT_CHUNKS, d).transpose(0, 2, 1, 3).reshape(batch, seq, d)
```

```python
import functools
import math

import jax
import jax.numpy as jnp
from jax import lax
from jax.experimental import pallas as pl
from jax.experimental.pallas import tpu as pltpu

F32 = jnp.float32
BF16 = jnp.bfloat16

SUBLANES = 8
LANES = 128
VMEM_BYTES_V7X = 64 * 1024 * 1024
VMEM_LIMIT = VMEM_BYTES_V7X - 8 * 1024 * 1024

GRID_W = 64
N_MOD = 9
RG_BLOCKS = 16
RG_CONV = 4
RG_PAD_LO = 2
SC_CONV = 3
SC_PAD_LO = 1
RG_C = 8.0
LN_EPS = 1e-5
N_PROJ = 7

SEG = GRID_W
ROW_TILE = SEG * SUBLANES
LAT_CHUNKS = SUBLANES
CTX_CHUNKS = 4
SCAN_BLOCK = 8
LN_ROWS = 16
MOD_K_SPLIT = 2
MOD_N_SPLIT = 2
LOG2_E = 1.0 / math.log(2.0)


def _sigmoid(x):
    return 1.0 / (1.0 + jnp.exp(-x))


def _sigmoid_tanh(x):
    return 0.5 + 0.5 * jnp.tanh(0.5 * x)


def _gelu_tanh(x):
    return x * (0.5 * (1.0 + jnp.tanh(math.sqrt(2.0 / math.pi) * (x + 0.044715 * (x * x * x)))))


def _layer_norm(z, g, b):
    mu = jnp.mean(z, axis=-1, keepdims=True)
    zc = z - mu
    var = jnp.mean(zc * zc, axis=-1, keepdims=True)
    return zc * lax.rsqrt(var + LN_EPS) * g + b


def _params(*sem):
    return pltpu.CompilerParams(dimension_semantics=sem, vmem_limit_bytes=VMEM_LIMIT)


def _mod_kernel(c_ref, *rest):
    w_refs, (b_ref, o_ref) = rest[:-2], rest[-2:]
    tk, nh = w_refs[0].shape

    @pl.when(pl.program_id(1) == 0)
    def _():
        o_ref[...] = jnp.broadcast_to(b_ref[...], o_ref.shape)

    c = c_ref[...]
    s = (c * _sigmoid(c)).astype(BF16)
    for q, w_ref in enumerate(w_refs):
        kq, hq = divmod(q, MOD_N_SPLIT)
        o_ref[:, hq * nh:(hq + 1) * nh] += jnp.dot(s[:, kq * tk:(kq + 1) * tk], w_ref[...].astype(BF16),
                                                   preferred_element_type=F32)


def _mod_vectors(cond, w_mod, b_mod):
    depth, d, n = w_mod.shape
    tk = LANES
    nh = n // MOD_N_SPLIT
    slab = lambda kq, hq: pl.BlockSpec((None, tk, nh), lambda l, k: (l, k * MOD_K_SPLIT + kq, hq))
    return pl.pallas_call(
        _mod_kernel,
        grid=(depth, d // (tk * MOD_K_SPLIT)),
        in_specs=[pl.BlockSpec((SUBLANES, tk * MOD_K_SPLIT), lambda l, k: (0, k))]
        + [slab(kq, hq) for kq in range(MOD_K_SPLIT) for hq in range(MOD_N_SPLIT)]
        + [pl.BlockSpec((None, 1, n), lambda l, k: (l, 0, 0))],
        out_specs=pl.BlockSpec((None, SUBLANES, n), lambda l, k: (l, 0, 0)),
        out_shape=jax.ShapeDtypeStruct((depth, SUBLANES, n), F32),
        compiler_params=_params("arbitrary", "arbitrary"),
        name="mod_vectors",
    )(cond, *([w_mod] * (MOD_K_SPLIT * MOD_N_SPLIT)), b_mod.reshape(depth, 1, n))


def _ffn_kernel(x_ref, xp_ref, mod_ref, modp_ref, wg_ref, wu_ref, wo_ref, lng_ref, lnb_ref, *rest,
                k0, alpha, emit_h, per_step):
    if emit_h:
        o_ref, h2_ref, h_scr, acc_even, acc_odd = rest
    else:
        o_ref, h_scr, acc_even, acc_odd = rest
    i = pl.program_id(0)
    f = pl.program_id(1)
    n_tiles = pl.num_programs(0) - 1
    n_sub = x_ref.shape[0] // LN_ROWS

    def finish_rows(acc_prev):
        m = modp_ref[...]
        half_gate = 0.5 * m[k0 + 2:k0 + 3]
        g = lng_ref[...]
        b = lnb_ref[...]
        if emit_h:
            scale = 1.0 + m[k0 + 4:k0 + 5]
            g2 = g * scale
            b2 = b * scale + m[k0 + 3:k0 + 4]
        for u in range(per_step):
            sb = jnp.minimum(f * per_step + u, n_sub - 1)
            rows = pl.ds(pl.multiple_of(sb * LN_ROWS, LN_ROWS), LN_ROWS)
            z = alpha * xp_ref[rows, :] + half_gate * acc_prev[rows, :]
            mu = jnp.mean(z, axis=-1, keepdims=True)
            zc = z - mu
            var = jnp.mean(zc * zc, axis=-1, keepdims=True)
            zn = zc * lax.rsqrt(var + LN_EPS)
            o_ref[rows, :] = zn * g + b
            if emit_h:
                h2_ref[rows, :] = (zn * g2 + b2).astype(BF16)

    def tile_step(acc_cur, acc_prev):
        @pl.when(f == 0)
        def _():
            m = mod_ref[...]
            h_scr[...] = (x_ref[...] * (1.0 + m[k0 + 1:k0 + 2]) + m[k0:k0 + 1]).astype(BF16)
            acc_cur[...] = jnp.zeros_like(acc_cur)

        h = h_scr[...]
        g = jnp.dot(h, wg_ref[...], preferred_element_type=F32)
        u = jnp.dot(h, wu_ref[...], preferred_element_type=F32)
        a = (g * _sigmoid(g) * u).astype(BF16)
        acc_cur[...] += jnp.dot(a, wo_ref[...], preferred_element_type=F32)
        finish_rows(acc_prev)

    @pl.when(jnp.logical_and(i == 0, f == 0))
    def _():
        acc_odd[...] = jnp.zeros_like(acc_odd)

    @pl.when(jnp.logical_and(i < n_tiles, i % 2 == 0))
    def _():
        tile_step(acc_even, acc_odd)

    @pl.when(jnp.logical_and(i < n_tiles, i % 2 == 1))
    def _():
        tile_step(acc_odd, acc_even)

    @pl.when(jnp.logical_and(i == n_tiles, n_tiles % 2 == 1))
    def _():
        finish_rows(acc_even)

    @pl.when(jnp.logical_and(i == n_tiles, n_tiles % 2 == 0))
    def _():
        finish_rows(acc_odd)


def _ffn_sublayer(x, mods, mod_row, w_in, w_out, ln_g, ln_b, *, layer, ln_idx, k0, alpha, emit_h):
    t, d = x.shape
    f_dim = w_out.shape[1]
    tm, tf = ROW_TILE, 512
    nf = f_dim // tf
    n_tiles = t // tm
    cur = lambda i: jnp.minimum(i, n_tiles - 1)
    prev = lambda i: jnp.maximum(i - 1, 0)
    fblk = lambda i, f: jnp.where(i == n_tiles, nf - 1, f)
    out_shape = [jax.ShapeDtypeStruct((t, d), F32)]
    out_specs = [pl.BlockSpec((tm, d), lambda i, f: (prev(i), 0))]
    if emit_h:
        out_shape.append(jax.ShapeDtypeStruct((t, d), BF16))
        out_specs.append(pl.BlockSpec((tm, d), lambda i, f: (prev(i), 0)))
    res = pl.pallas_call(
        functools.partial(_ffn_kernel, k0=k0, alpha=alpha, emit_h=emit_h,
                          per_step=pl.cdiv(tm // LN_ROWS, nf)),
        grid=(n_tiles + 1, nf),
        in_specs=[
            pl.BlockSpec((tm, d), lambda i, f: (cur(i), 0)),
            pl.BlockSpec((tm, d), lambda i, f: (prev(i), 0)),
            pl.BlockSpec((None, None, N_MOD, d), lambda i, f: (layer, mod_row(cur(i)), 0, 0)),
            pl.BlockSpec((None, None, N_MOD, d), lambda i, f: (layer, mod_row(prev(i)), 0, 0)),
            pl.BlockSpec((None, d, tf), lambda i, f: (layer, 0, fblk(i, f))),
            pl.BlockSpec((None, d, tf), lambda i, f: (layer, 0, fblk(i, f) + nf)),
            pl.BlockSpec((None, tf, d), lambda i, f: (layer, fblk(i, f), 0)),
            pl.BlockSpec((None, 1, d), lambda i, f: (ln_idx, 0, 0)),
            pl.BlockSpec((None, 1, d), lambda i, f: (ln_idx, 0, 0)),
        ],
        out_specs=out_specs,
        out_shape=out_shape,
        scratch_shapes=[pltpu.VMEM((tm, d), BF16), pltpu.VMEM((tm, d), F32), pltpu.VMEM((tm, d), F32)],
        compiler_params=_params("arbitrary", "arbitrary"),
        name="ffn_sublayer",
    )(x, x, mods, mods, w_in, w_in, w_out, ln_g, ln_b)
    return res if emit_h else res[0]


def _matmul_kernel(a_ref, b_ref, o_ref):
    o_ref[...] = jnp.dot(a_ref[...], b_ref[...], preferred_element_type=F32).astype(o_ref.dtype)


def _project(h, w, n_cols, *, layer):
    t, d = h.shape
    tm = min(t, 1024)
    tn = 2048
    return pl.pallas_call(
        _matmul_kernel,
        grid=(t // tm, n_cols // tn),
        in_specs=[
            pl.BlockSpec((tm, d), lambda i, j: (i, 0)),
            pl.BlockSpec((None, d, tn), lambda i, j: (layer, 0, j)),
        ],
        out_specs=pl.BlockSpec((tm, tn), lambda i, j: (i, j)),
        out_shape=jax.ShapeDtypeStruct((t, n_cols), BF16),
        compiler_params=_params("arbitrary", "arbitrary"),
        name="in_proj",
    )(h, w)


def _fill_halo(scr, x3, lo, hi, seq_chunks):
    n = x3.shape[0]
    scr[lo:lo + n] = x3
    tile = x3.shape[1:]
    if seq_chunks is None:
        if lo:
            scr[0:lo] = jnp.zeros((lo,) + tile, F32)
        if hi:
            scr[lo + n:lo + n + hi] = jnp.zeros((hi,) + tile, F32)
        return
    chunk = lax.broadcasted_iota(jnp.int32, tile, 0) % seq_chunks
    for t in range(lo):
        prev = pltpu.roll(x3[n - lo + t], 1, 0)
        scr[t] = jnp.where(chunk != 0, prev, 0.0)
    for t in range(hi):
        nxt = pltpu.roll(x3[t], SUBLANES - 1, 0)
        scr[lo + n + t] = jnp.where(chunk != seq_chunks - 1, nxt, 0.0)


def _conv_from_halo(scr, w, n):
    out = w[0:1] * scr[0:n]
    for k in range(1, w.shape[0]):
        out = out + w[k:k + 1] * scr[k:k + n]
    return out


def _rglru_kernel(p_ref, cw_ref, cb_ref, gw_ref, gb_ref, lam_ref, h0f_ref, h0b_ref, *rest,
                  chunk_len, seq_chunks, wrap, emit_state):
    if emit_state:
        h_ref, sf_ref, sb_ref, x_scr, a_f, v_f, a_b, v_b, h_f, p_f, h_b, p_b = rest
    else:
        h_ref, x_scr, a_f, v_f, a_b, v_b, h_f, p_f, h_b, p_b = rest
    n_seg = chunk_len // SEG
    cw = cw_ref[...]
    cb = cb_ref[...]
    lam = lam_ref[...]
    log_sig = jnp.minimum(lam, 0.0) - jnp.log1p(jnp.exp(-jnp.abs(lam)))
    e_fac = (0.5 * RG_C * LOG2_E) * log_sig
    gw_half = gw_ref[...] * 0.5
    gb_half = gb_ref[...] * 0.5
    a_scr = (a_f, a_b)
    v_scr = (v_f, v_b)

    def gates(si, _):
        r0 = pl.multiple_of(si * ROW_TILE, ROW_TILE)
        x3 = p_ref[pl.ds(r0, ROW_TILE), :].astype(F32).reshape(SEG, SUBLANES, LANES)
        _fill_halo(x_scr, x3, RG_PAD_LO, RG_CONV - 1 - RG_PAD_LO, seq_chunks if wrap else None)
        xc = (_conv_from_halo(x_scr, cw, SEG) + cb).reshape(ROW_TILE, LANES)
        t = jnp.tanh(jnp.dot(xc.astype(BF16), gw_half, preferred_element_type=F32) + gb_half)
        xh = 0.5 * xc
        j0 = pl.multiple_of(si * SEG, SEG)
        for d in range(2):
            t_r = t[:, (2 * d) * LANES:(2 * d + 1) * LANES]
            t_i = t[:, (2 * d + 1) * LANES:(2 * d + 2) * LANES]
            ef = e_fac[d:d + 1]
            a = jnp.exp2(ef * t_r + ef)
            y = 1.0 - a * a
            s = jnp.where(y > 0.0, y * lax.rsqrt(y), 0.0)
            v = s * (xh * t_i + xh)
            a_scr[d][pl.ds(j0, SEG)] = a.reshape(SEG, SUBLANES, LANES)
            v_scr[d][pl.ds(j0, SEG)] = v.reshape(SEG, SUBLANES, LANES)
        return 0

    lax.fori_loop(0, n_seg, gates, 0)

    def steps(blk, carry):
        hf, pf, hb, pb = carry
        jf = pl.ds(pl.multiple_of(blk * SCAN_BLOCK, SCAN_BLOCK), SCAN_BLOCK)
        jb = pl.ds(pl.multiple_of(chunk_len - SCAN_BLOCK - blk * SCAN_BLOCK, SCAN_BLOCK), SCAN_BLOCK)
        a, v = a_f[jf], v_f[jf]
        hs, ps = [], []
        for u in range(SCAN_BLOCK):
            hf = a[u] * hf + v[u]
            pf = a[u] * pf
            hs.append(hf)
            ps.append(pf)
        h_f[jf] = jnp.stack(hs)
        p_f[jf] = jnp.stack(ps)
        a, v = a_b[jb], v_b[jb]
        hs, ps = [], []
        for u in reversed(range(SCAN_BLOCK)):
            hb = a[u] * hb + v[u]
            pb = a[u] * pb
            hs.append(hb)
            ps.append(pb)
        h_b[jb] = jnp.stack(hs[::-1])
        p_b[jb] = jnp.stack(ps[::-1])
        return hf, pf, hb, pb

    zero = jnp.zeros((SUBLANES, LANES), F32)
    one = jnp.ones((SUBLANES, LANES), F32)
    hf, pf, hb, pb = lax.fori_loop(0, chunk_len // SCAN_BLOCK, steps, (zero, one, zero, one))

    sub = lax.broadcasted_iota(jnp.int32, (SUBLANES, LANES), 0)
    h0f = h0f_ref[...]
    h0b = h0b_ref[...]
    cf = jnp.zeros((SUBLANES, LANES), F32)
    row = None
    for s in range(SUBLANES):
        if s % seq_chunks == 0:
            row = h0f[s:s + 1]
        else:
            row = hf[s - 1:s] + pf[s - 1:s] * row
        cf = jnp.where(sub == s, row, cf)
    cbk = jnp.zeros((SUBLANES, LANES), F32)
    for s in reversed(range(SUBLANES)):
        if s % seq_chunks == seq_chunks - 1:
            row = h0b[s:s + 1]
        else:
            row = hb[s + 1:s + 2] + pb[s + 1:s + 2] * row
        cbk = jnp.where(sub == s, row, cbk)

    if emit_state:
        sf_ref[...] = hf + pf * cf
        sb_ref[...] = hb + pb * cbk

    def combine(si, _):
        j0 = pl.multiple_of(si * SEG, SEG)
        r0 = pl.multiple_of(si * ROW_TILE, ROW_TILE)
        js = pl.ds(j0, SEG)
        h = (h_f[js] + p_f[js] * cf) + (h_b[js] + p_b[js] * cbk)
        h_ref[pl.ds(r0, ROW_TILE), :] = h.reshape(ROW_TILE, LANES).astype(h_ref.dtype)
        return 0

    lax.fori_loop(0, n_seg, combine, 0)


def _rglru_branch(p, conv_w, conv_b, gate_w, gate_b, lam, h0f, h0b, *, layer, chunk_len, seq_chunks, wrap,
                  emit_state):
    t = p.shape[0]
    w = conv_w.shape[2]
    rows = chunk_len * SUBLANES
    groups = t // rows
    cb = LANES
    out_shape = [jax.ShapeDtypeStruct((t, w), BF16)]
    out_specs = [pl.BlockSpec((rows, cb), lambda g, c: (g, c))]
    if emit_state:
        for _ in range(2):
            out_shape.append(jax.ShapeDtypeStruct((groups, SUBLANES, w), F32))
            out_specs.append(pl.BlockSpec((None, SUBLANES, cb), lambda g, c: (g, 0, c)))
    return pl.pallas_call(
        functools.partial(_rglru_kernel, chunk_len=chunk_len, seq_chunks=seq_chunks, wrap=wrap,
                          emit_state=emit_state),
        grid=(groups, w // cb),
        in_specs=[
            pl.BlockSpec((rows, cb), lambda g, c: (g, c)),
            pl.BlockSpec((None, RG_CONV, cb), lambda g, c: (layer, 0, c)),
            pl.BlockSpec((None, 1, cb), lambda g, c: (layer, 0, c)),
            pl.BlockSpec((None, None, cb, 4 * cb), lambda g, c: (layer, c, 0, 0)),
            pl.BlockSpec((None, None, 1, 4 * cb), lambda g, c: (layer, c, 0, 0)),
            pl.BlockSpec((None, 2, cb), lambda g, c: (layer, 0, c)),
            pl.BlockSpec((None, SUBLANES, cb), lambda g, c: (g, 0, c)),
            pl.BlockSpec((None, SUBLANES, cb), lambda g, c: (g, 0, c)),
        ],
        out_specs=out_specs,
        out_shape=out_shape,
        scratch_shapes=[pltpu.VMEM((SEG + RG_CONV - 1, SUBLANES, cb), F32)]
        + [pltpu.VMEM((chunk_len, SUBLANES, cb), F32) for _ in range(8)],
        compiler_params=_params("arbitrary", "arbitrary"),
        name="rglru_branch",
    )(p, conv_w, conv_b, gate_w, gate_b, lam, h0f, h0b)


def _merge_kernel(hrg_ref, gate_ref, scb_ref, scc_ref, scx_ref, grg_ref, gsc_ref, scw_ref, bm_ref,
                  wrg_ref, wsc_ref, o_ref, a_scr, s_scr, q_scr, acc_rg, acc_sc, *, seq_chunks, wrap):
    k = pl.program_id(1)
    nk = pl.num_programs(1) - 1
    tc = a_scr.shape[2]

    def produce(slot):
        a_scr[slot] = (hrg_ref[...].astype(F32) * _gelu_tanh(gate_ref[...].astype(F32))).astype(BF16)
        q = scc_ref[...].astype(F32) * scx_ref[...].astype(F32)
        _fill_halo(q_scr, q.reshape(SEG, SUBLANES, tc), SC_PAD_LO, SC_CONV - 1 - SC_PAD_LO,
                   seq_chunks if wrap else None)
        cq = _conv_from_halo(q_scr, scw_ref[...], SEG).reshape(ROW_TILE, tc)
        s_scr[slot] = (scb_ref[...].astype(F32) * cq).astype(BF16)

    def consume(slot):
        acc_rg[...] += jnp.dot(a_scr[slot], wrg_ref[...], preferred_element_type=F32)
        acc_sc[...] += jnp.dot(s_scr[slot], wsc_ref[...], preferred_element_type=F32)

    @pl.when(k == 0)
    def _():
        acc_rg[...] = jnp.zeros_like(acc_rg)
        acc_sc[...] = jnp.zeros_like(acc_sc)
        produce(0)

    @pl.when(jnp.logical_and(k > 0, k < nk))
    def _():
        produce(k % 2)
        consume((k + 1) % 2)

    @pl.when(k == nk)
    def _():
        consume((k + 1) % 2)
        bm = bm_ref[...]
        m = (_sigmoid_tanh(grg_ref[...].astype(F32) + bm[0:1]) * acc_rg[...]
             + _sigmoid_tanh(gsc_ref[...].astype(F32) + bm[1:2]) * acc_sc[...])
        o_ref[...] = m.astype(o_ref.dtype)


def _mixer_merge(p, h_rg, sc_conv_w, b_merge, w_rg_out, w_sc_out, *, layer, seq_chunks, wrap):
    t, d = h_rg.shape
    tm, tc = ROW_TILE, 512
    nb = d // tc
    chunk = lambda sec: pl.BlockSpec((tm, tc), lambda i, k: (i, sec * nb + jnp.minimum(k, nb - 1)))
    wchunk = pl.BlockSpec((None, tc, d), lambda i, k: (layer, jnp.maximum(k - 1, 0), 0))
    return pl.pallas_call(
        functools.partial(_merge_kernel, seq_chunks=seq_chunks, wrap=wrap),
        grid=(t // tm, nb + 1),
        in_specs=[
            chunk(0), chunk(1), chunk(2), chunk(3), chunk(4),
            pl.BlockSpec((tm, d), lambda i, k: (i, 5)),
            pl.BlockSpec((tm, d), lambda i, k: (i, 6)),
            pl.BlockSpec((None, SC_CONV, tc), lambda i, k: (layer, 0, jnp.minimum(k, nb - 1))),
            pl.BlockSpec((None, 2, d), lambda i, k: (layer, 0, 0)),
            wchunk, wchunk,
        ],
        out_specs=pl.BlockSpec((tm, d), lambda i, k: (i, 0)),
        out_shape=jax.ShapeDtypeStruct((t, d), BF16),
        scratch_shapes=[pltpu.VMEM((2, tm, tc), BF16), pltpu.VMEM((2, tm, tc), BF16),
                        pltpu.VMEM((SEG + SC_CONV - 1, SUBLANES, tc), F32),
                        pltpu.VMEM((tm, d), F32), pltpu.VMEM((tm, d), F32)],
        compiler_params=_params("arbitrary", "arbitrary"),
        name="mixer_merge",
    )(h_rg, p, p, p, p, p, p, sc_conv_w, b_merge, w_rg_out, w_sc_out)


def _oproj_kernel(m_ref, x_ref, mod_ref, wo_ref, lng_ref, lnb_ref, o_ref, *, alpha):
    y = jnp.dot(m_ref[...], wo_ref[...], preferred_element_type=F32)
    z = alpha * x_ref[...] + mod_ref[...][5:6] * y
    o_ref[...] = _layer_norm(z, lng_ref[...], lnb_ref[...])


def _mixer_oproj(m, x, mods, mod_row, w_o, ln_g, ln_b, *, layer, ln_idx, alpha):
    t, d = x.shape
    tm = ROW_TILE
    return pl.pallas_call(
        functools.partial(_oproj_kernel, alpha=alpha),
        grid=(t // tm,),
        in_specs=[
            pl.BlockSpec((tm, d), lambda i: (i, 0)),
            pl.BlockSpec((tm, d), lambda i: (i, 0)),
            pl.BlockSpec((None, None, N_MOD, d), lambda i: (layer, mod_row(i), 0, 0)),
            pl.BlockSpec((None, d, d), lambda i: (layer, 0, 0)),
            pl.BlockSpec((None, 1, d), lambda i: (ln_idx, 0, 0)),
            pl.BlockSpec((None, 1, d), lambda i: (ln_idx, 0, 0)),
        ],
        out_specs=pl.BlockSpec((tm, d), lambda i: (i, 0)),
        out_shape=jax.ShapeDtypeStruct((t, d), F32),
        compiler_params=_params("arbitrary"),
        name="mixer_oproj",
    )(m, x, mods, w_o, ln_g, ln_b)


def kernel(x, c, ctx, c_ctx, w_mod, b_mod, ln_g, ln_b, ffn1_w_in, ffn1_w_out, ffn2_w_in, ffn2_w_out, w_in, rg_conv_w, rg_conv_b, rg_gate_w, rg_gate_b, rg_lam, sc_conv_w, w_rg_out, w_sc_out, b_merge, w_o):
    batch, seq, d = x.shape
    ctx_len = ctx.shape[1]
    depth = w_mod.shape[0]
    w_rg = rg_conv_w.shape[2]
    n_ln = ln_g.shape[1]
    alpha = (2 * depth) ** 0.25
    lat_chunk = seq // LAT_CHUNKS
    ctx_chunk = ctx_len // CTX_CHUNKS
    ctx_group = SUBLANES // CTX_CHUNKS
    assert lat_chunk % GRID_W == 0 and ctx_chunk == SEG and batch % ctx_group == 0
    assert w_rg == RG_BLOCKS * LANES and d % LANES == 0

    xl = x.reshape(batch, LAT_CHUNKS, lat_chunk, d).transpose(0, 2, 1, 3).reshape(batch * seq, d)
    xc = ctx.reshape(batch // ctx_group, ctx_group, CTX_CHUNKS, ctx_chunk, d).transpose(0, 3, 1, 2, 4)
    xc = xc.reshape(batch * ctx_len, d)

    cond = jnp.zeros((SUBLANES, d), F32).at[:batch].set(c).at[batch].set(c_ctx)
    mods = _mod_vectors(cond, w_mod, b_mod).reshape(depth, SUBLANES, N_MOD, d)
    lat_tiles = seq // ROW_TILE
    lat_row = lambda i: i // lat_tiles
    ctx_row = lambda i: batch

    bf = lambda a: a.astype(BF16)
    f1_in, f1_out, f2_in, f2_out = bf(ffn1_w_in), bf(ffn1_w_out), bf(ffn2_w_in), bf(ffn2_w_out)
    w_in_b, w_rg_out_b, w_sc_out_b, w_o_b = bf(w_in), bf(w_rg_out), bf(w_sc_out), bf(w_o)
    ln_g = ln_g.reshape(depth * n_ln, 1, d)
    ln_b = ln_b.reshape(depth * n_ln, 1, d)
    gw = bf(rg_gate_w.transpose(0, 3, 4, 1, 2, 5).reshape(depth, RG_BLOCKS, LANES, 4 * LANES))
    gb = rg_gate_b.reshape(depth, 4, RG_BLOCKS, LANES).transpose(0, 2, 1, 3).reshape(depth, RG_BLOCKS, 1, 4 * LANES)
    rg_args = (rg_conv_w, rg_conv_b.reshape(depth, 1, w_rg), gw, gb, rg_lam)
    zeros_state = jnp.zeros((batch // ctx_group, SUBLANES, w_rg), F32)

    for l in range(depth):
        last = l == depth - 1
        ffn = functools.partial(_ffn_sublayer, layer=l, alpha=alpha)
        ln = lambda k: dict(ln_g=ln_g, ln_b=ln_b, ln_idx=l * n_ln + k)

        xl, hl = ffn(xl, mods, lat_row, f1_in, f1_out, **ln(0), k0=0, emit_h=True)
        xc, hc = ffn(xc, mods, ctx_row, f1_in, f1_out, **ln(0), k0=0, emit_h=True)

        pc = _project(hc, w_in_b, w_rg if last else N_PROJ * d, layer=l)
        hrg_c, s_f, s_b = _rglru_branch(pc, *rg_args, zeros_state, zeros_state, layer=l, chunk_len=ctx_chunk,
                                        seq_chunks=CTX_CHUNKS, wrap=True, emit_state=True)
        if not last:
            mc = _mixer_merge(pc, hrg_c, sc_conv_w, b_merge, w_rg_out_b, w_sc_out_b, layer=l,
                              seq_chunks=CTX_CHUNKS, wrap=True)
            xc = _mixer_oproj(mc, xc, mods, ctx_row, w_o_b, **ln(1), layer=l, alpha=alpha)
        s_f = s_f.reshape(batch, CTX_CHUNKS, w_rg)[:, CTX_CHUNKS - 1]
        s_b = s_b.reshape(batch, CTX_CHUNKS, w_rg)[:, 0]
        h0f = jnp.broadcast_to(s_f[:, None, :], (batch, SUBLANES, w_rg))
        h0b = jnp.broadcast_to(s_b[:, None, :], (batch, SUBLANES, w_rg))

        pl_ = _project(hl, w_in_b, N_PROJ * d, layer=l)
        (hrg_l,) = _rglru_branch(pl_, *rg_args, h0f, h0b, layer=l, chunk_len=lat_chunk, seq_chunks=LAT_CHUNKS,
                                 wrap=False, emit_state=False)
        ml = _mixer_merge(pl_, hrg_l, sc_conv_w, b_merge, w_rg_out_b, w_sc_out_b, layer=l,
                          seq_chunks=LAT_CHUNKS, wrap=False)
        xl = _mixer_oproj(ml, xl, mods, lat_row, w_o_b, **ln(1), layer=l, alpha=alpha)

        xl = ffn(xl, mods, lat_row, f2_in, f2_out, **ln(2), k0=6, emit_h=False)
        if not last:
            xc = ffn(xc, mods, ctx_row, f2_in, f2_out, **ln(2), k0=6, emit_h=False)

    return xl.reshape(batch, lat_chunk, LAT_CHUNKS, d).transpose(0, 2, 1, 3).reshape(batch, seq, d)
```

```python
import functools
import math

import jax
import jax.numpy as jnp
from jax import lax
from jax.experimental import pallas as pl
from jax.experimental.pallas import tpu as pltpu

F32 = jnp.float32
BF16 = jnp.bfloat16

SUBLANES = 8
LANES = 128
VMEM_BYTES_V7X = 64 * 1024 * 1024
VMEM_LIMIT = VMEM_BYTES_V7X - 8 * 1024 * 1024

GRID_W = 64
N_MOD = 9
RG_BLOCKS = 16
RG_CONV = 4
RG_PAD_LO = 2
SC_CONV = 3
SC_PAD_LO = 1
RG_C = 8.0
LN_EPS = 1e-5
N_PROJ = 7

SEG = GRID_W
ROW_TILE = SEG * SUBLANES
LAT_CHUNKS = SUBLANES
CTX_CHUNKS = 4
SCAN_BLOCK = 8
LN_ROWS = 128
FFN_ROW_TILE = 2 * ROW_TILE


def _sigmoid(x):
    return 1.0 / (1.0 + jnp.exp(-x))


def _sigmoid_tanh(x):
    return 0.5 + 0.5 * jnp.tanh(0.5 * x)


def _gelu_tanh(x):
    return x * (0.5 * (1.0 + jnp.tanh(math.sqrt(2.0 / math.pi) * (x + 0.044715 * (x * x * x)))))


def _layer_norm(z, g, b):
    mu = jnp.mean(z, axis=-1, keepdims=True)
    zc = z - mu
    var = jnp.mean(zc * zc, axis=-1, keepdims=True)
    return zc * lax.rsqrt(var + LN_EPS) * g + b


def _params(*sem):
    return pltpu.CompilerParams(dimension_semantics=sem, vmem_limit_bytes=VMEM_LIMIT)


def _mod_kernel(c_ref, w_ref, b_ref, o_ref):
    @pl.when(pl.program_id(1) == 0)
    def _():
        o_ref[...] = jnp.broadcast_to(b_ref[...], o_ref.shape)

    c = c_ref[...]
    s = (c * _sigmoid(c)).astype(BF16)
    o_ref[...] += jnp.dot(s, w_ref[...].astype(BF16), preferred_element_type=F32)


def _mod_vectors(cond, w_mod, b_mod):
    depth, d, n = w_mod.shape
    tk = LANES
    return pl.pallas_call(
        _mod_kernel,
        grid=(depth, d // tk),
        in_specs=[
            pl.BlockSpec((SUBLANES, tk), lambda l, k: (0, k)),
            pl.BlockSpec((None, tk, n), lambda l, k: (l, k, 0)),
            pl.BlockSpec((None, 1, n), lambda l, k: (l, 0, 0)),
        ],
        out_specs=pl.BlockSpec((None, SUBLANES, n), lambda l, k: (l, 0, 0)),
        out_shape=jax.ShapeDtypeStruct((depth, SUBLANES, n), F32),
        compiler_params=_params("arbitrary", "arbitrary"),
        name="mod_vectors",
    )(cond, w_mod, b_mod.reshape(depth, 1, n))


def _ffn_kernel(x_ref, mod_ref, wg_ref, wu_ref, wo_ref, lng_ref, lnb_ref, *rest, k0, alpha, emit_h):
    if emit_h:
        o_ref, h2_ref, h_scr = rest
    else:
        o_ref, h_scr = rest
    f = pl.program_id(1)

    @pl.when(f == 0)
    def _():
        m = mod_ref[...]
        h_scr[...] = (x_ref[...] * (1.0 + m[k0 + 1:k0 + 2]) + m[k0:k0 + 1]).astype(BF16)
        o_ref[...] = jnp.zeros_like(o_ref)

    h = h_scr[...]
    g = jnp.dot(h, wg_ref[...], preferred_element_type=F32)
    u = jnp.dot(h, wu_ref[...], preferred_element_type=F32)
    a = (g * _sigmoid(g) * u).astype(BF16)
    o_ref[...] += jnp.dot(a, wo_ref[...], preferred_element_type=F32)

    @pl.when(f == pl.num_programs(1) - 1)
    def _():
        m = mod_ref[...]
        half_gate = 0.5 * m[k0 + 2:k0 + 3]
        g_ln = lng_ref[...]
        b_ln = lnb_ref[...]
        if emit_h:
            scale = 1.0 + m[k0 + 4:k0 + 5]
            g2 = g_ln * scale
            b2 = b_ln * scale + m[k0 + 3:k0 + 4]

        def finish_rows(sb, _):
            rows = pl.ds(pl.multiple_of(sb * LN_ROWS, LN_ROWS), LN_ROWS)
            z = alpha * x_ref[rows, :] + half_gate * o_ref[rows, :]
            mu = jnp.mean(z, axis=-1, keepdims=True)
            zc = z - mu
            var = jnp.mean(zc * zc, axis=-1, keepdims=True)
            zn = zc * lax.rsqrt(var + LN_EPS)
            o_ref[rows, :] = zn * g_ln + b_ln
            if emit_h:
                h2_ref[rows, :] = (zn * g2 + b2).astype(BF16)
            return 0

        lax.fori_loop(0, x_ref.shape[0] // LN_ROWS, finish_rows, 0)


def _ffn_sublayer(x, mods, mod_row, w_in, w_out, ln_g, ln_b, *, layer, ln_idx, k0, alpha, emit_h):
    t, d = x.shape
    f_dim = w_out.shape[1]
    tm, tf = min(FFN_ROW_TILE, t), 256
    nf = f_dim // tf
    out_shape = [jax.ShapeDtypeStruct((t, d), F32)]
    out_specs = [pl.BlockSpec((tm, d), lambda i, f: (i, 0))]
    if emit_h:
        out_shape.append(jax.ShapeDtypeStruct((t, d), BF16))
        out_specs.append(pl.BlockSpec((tm, d), lambda i, f: (i, 0)))
    res = pl.pallas_call(
        functools.partial(_ffn_kernel, k0=k0, alpha=alpha, emit_h=emit_h),
        grid=(t // tm, nf),
        in_specs=[
            pl.BlockSpec((tm, d), lambda i, f: (i, 0)),
            pl.BlockSpec((None, None, N_MOD, d), lambda i, f: (layer, mod_row(i * tm), 0, 0)),
            pl.BlockSpec((None, d, tf), lambda i, f: (layer, 0, f)),
            pl.BlockSpec((None, d, tf), lambda i, f: (layer, 0, f + nf)),
            pl.BlockSpec((None, tf, d), lambda i, f: (layer, f, 0)),
            pl.BlockSpec((None, 1, d), lambda i, f: (ln_idx, 0, 0)),
            pl.BlockSpec((None, 1, d), lambda i, f: (ln_idx, 0, 0)),
        ],
        out_specs=out_specs,
        out_shape=out_shape,
        scratch_shapes=[pltpu.VMEM((tm, d), BF16)],
        compiler_params=_params("arbitrary", "arbitrary"),
        name="ffn_sublayer",
    )(x, mods, w_in, w_in, w_out, ln_g, ln_b)
    return res if emit_h else res[0]


def _matmul_kernel(a_ref, b_ref, o_ref):
    o_ref[...] = jnp.dot(a_ref[...], b_ref[...], preferred_element_type=F32).astype(o_ref.dtype)


def _project(h, w, n_cols, *, layer):
    t, d = h.shape
    tm = min(t, 1024)
    tn = 2048
    return pl.pallas_call(
        _matmul_kernel,
        grid=(t // tm, n_cols // tn),
        in_specs=[
            pl.BlockSpec((tm, d), lambda i, j: (i, 0)),
            pl.BlockSpec((None, d, tn), lambda i, j: (layer, 0, j)),
        ],
        out_specs=pl.BlockSpec((tm, tn), lambda i, j: (i, j)),
        out_shape=jax.ShapeDtypeStruct((t, n_cols), BF16),
        compiler_params=_params("arbitrary", "arbitrary"),
        name="in_proj",
    )(h, w)


def _fill_halo(scr, x3, lo, hi, seq_chunks):
    n = x3.shape[0]
    scr[lo:lo + n] = x3
    tile = x3.shape[1:]
    if seq_chunks is None:
        if lo:
            scr[0:lo] = jnp.zeros((lo,) + tile, F32)
        if hi:
            scr[lo + n:lo + n + hi] = jnp.zeros((hi,) + tile, F32)
        return
    chunk = lax.broadcasted_iota(jnp.int32, tile, 0) % seq_chunks
    for t in range(lo):
        prev = pltpu.roll(x3[n - lo + t], 1, 0)
        scr[t] = jnp.where(chunk != 0, prev, 0.0)
    for t in range(hi):
        nxt = pltpu.roll(x3[t], SUBLANES - 1, 0)
        scr[lo + n + t] = jnp.where(chunk != seq_chunks - 1, nxt, 0.0)


def _conv_from_halo(scr, w, n):
    out = w[0:1] * scr[0:n]
    for k in range(1, w.shape[0]):
        out = out + w[k:k + 1] * scr[k:k + n]
    return out


def _rglru_kernel(p_ref, cw_ref, cb_ref, gw_ref, gb_ref, lam_ref, h0f_ref, h0b_ref, *rest,
                  chunk_len, seq_chunks, wrap, emit_state):
    if emit_state:
        h_ref, sf_ref, sb_ref, x_scr, a_f, v_f, a_b, v_b, h_f, p_f, h_b, p_b = rest
    else:
        h_ref, x_scr, a_f, v_f, a_b, v_b, h_f, p_f, h_b, p_b = rest
    n_seg = chunk_len // SEG
    cw = cw_ref[...]
    cb = cb_ref[...]
    lam = lam_ref[...]
    log_sig = jnp.minimum(lam, 0.0) - jnp.log1p(jnp.exp(-jnp.abs(lam)))
    e_fac = (0.5 * RG_C) * log_sig
    gw_half = gw_ref[...] * 0.5
    gb_half = gb_ref[...] * 0.5
    a_scr = (a_f, a_b)
    v_scr = (v_f, v_b)

    def gates(si, _):
        r0 = pl.multiple_of(si * ROW_TILE, ROW_TILE)
        x3 = p_ref[pl.ds(r0, ROW_TILE), :].astype(F32).reshape(SEG, SUBLANES, LANES)
        _fill_halo(x_scr, x3, RG_PAD_LO, RG_CONV - 1 - RG_PAD_LO, seq_chunks if wrap else None)
        xc = (_conv_from_halo(x_scr, cw, SEG) + cb).reshape(ROW_TILE, LANES)
        t = jnp.tanh(jnp.dot(xc.astype(BF16), gw_half, preferred_element_type=F32) + gb_half)
        xh = 0.5 * xc
        j0 = pl.multiple_of(si * SEG, SEG)
        for d in range(2):
            t_r = t[:, (2 * d) * LANES:(2 * d + 1) * LANES]
            t_i = t[:, (2 * d + 1) * LANES:(2 * d + 2) * LANES]
            ef = e_fac[d:d + 1]
            a = jnp.exp(ef * t_r + ef)
            y = 1.0 - a * a
            s = jnp.where(y > 0.0, y * lax.rsqrt(y), 0.0)
            v = s * (xh * t_i + xh)
            a_scr[d][pl.ds(j0, SEG)] = a.reshape(SEG, SUBLANES, LANES)
            v_scr[d][pl.ds(j0, SEG)] = v.reshape(SEG, SUBLANES, LANES)
        return 0

    lax.fori_loop(0, n_seg, gates, 0)

    def steps(blk, carry):
        hf, pf, hb, pb = carry
        jf = pl.ds(pl.multiple_of(blk * SCAN_BLOCK, SCAN_BLOCK), SCAN_BLOCK)
        jb = pl.ds(pl.multiple_of(chunk_len - SCAN_BLOCK - blk * SCAN_BLOCK, SCAN_BLOCK), SCAN_BLOCK)
        a, v = a_f[jf], v_f[jf]
        hs, ps = [], []
        for u in range(SCAN_BLOCK):
            hf = a[u] * hf + v[u]
            pf = a[u] * pf
            hs.append(hf)
            ps.append(pf)
        h_f[jf] = jnp.stack(hs)
        p_f[jf] = jnp.stack(ps)
        a, v = a_b[jb], v_b[jb]
        hs, ps = [], []
        for u in reversed(range(SCAN_BLOCK)):
            hb = a[u] * hb + v[u]
            pb = a[u] * pb
            hs.append(hb)
            ps.append(pb)
        h_b[jb] = jnp.stack(hs[::-1])
        p_b[jb] = jnp.stack(ps[::-1])
        return hf, pf, hb, pb

    zero = jnp.zeros((SUBLANES, LANES), F32)
    one = jnp.ones((SUBLANES, LANES), F32)
    hf, pf, hb, pb = lax.fori_loop(0, chunk_len // SCAN_BLOCK, steps, (zero, one, zero, one))

    sub = lax.broadcasted_iota(jnp.int32, (SUBLANES, LANES), 0)
    h0f = h0f_ref[...]
    h0b = h0b_ref[...]
    cf = jnp.zeros((SUBLANES, LANES), F32)
    row = None
    for s in range(SUBLANES):
        if s % seq_chunks == 0:
            row = h0f[s:s + 1]
        else:
            row = hf[s - 1:s] + pf[s - 1:s] * row
        cf = jnp.where(sub == s, row, cf)
    cbk = jnp.zeros((SUBLANES, LANES), F32)
    for s in reversed(range(SUBLANES)):
        if s % seq_chunks == seq_chunks - 1:
            row = h0b[s:s + 1]
        else:
            row = hb[s + 1:s + 2] + pb[s + 1:s + 2] * row
        cbk = jnp.where(sub == s, row, cbk)

    if emit_state:
        sf_ref[...] = hf + pf * cf
        sb_ref[...] = hb + pb * cbk

    def combine(si, _):
        j0 = pl.multiple_of(si * SEG, SEG)
        r0 = pl.multiple_of(si * ROW_TILE, ROW_TILE)
        js = pl.ds(j0, SEG)
        h = (h_f[js] + p_f[js] * cf) + (h_b[js] + p_b[js] * cbk)
        h_ref[pl.ds(r0, ROW_TILE), :] = h.reshape(ROW_TILE, LANES).astype(h_ref.dtype)
        return 0

    lax.fori_loop(0, n_seg, combine, 0)


def _rglru_branch(p, conv_w, conv_b, gate_w, gate_b, lam, h0f, h0b, *, layer, chunk_len, seq_chunks, wrap,
                  emit_state):
    t = p.shape[0]
    w = conv_w.shape[2]
    rows = chunk_len * SUBLANES
    groups = t // rows
    cb = LANES
    out_shape = [jax.ShapeDtypeStruct((t, w), BF16)]
    out_specs = [pl.BlockSpec((rows, cb), lambda g, c: (g, c))]
    if emit_state:
        for _ in range(2):
            out_shape.append(jax.ShapeDtypeStruct((groups, SUBLANES, w), F32))
            out_specs.append(pl.BlockSpec((None, SUBLANES, cb), lambda g, c: (g, 0, c)))
    return pl.pallas_call(
        functools.partial(_rglru_kernel, chunk_len=chunk_len, seq_chunks=seq_chunks, wrap=wrap,
                          emit_state=emit_state),
        grid=(groups, w // cb),
        in_specs=[
            pl.BlockSpec((rows, cb), lambda g, c: (g, c)),
            pl.BlockSpec((None, RG_CONV, cb), lambda g, c: (layer, 0, c)),
            pl.BlockSpec((None, 1, cb), lambda g, c: (layer, 0, c)),
            pl.BlockSpec((None, None, cb, 4 * cb), lambda g, c: (layer, c, 0, 0)),
            pl.BlockSpec((None, None, 1, 4 * cb), lambda g, c: (layer, c, 0, 0)),
            pl.BlockSpec((None, 2, cb), lambda g, c: (layer, 0, c)),
            pl.BlockSpec((None, SUBLANES, cb), lambda g, c: (g, 0, c)),
            pl.BlockSpec((None, SUBLANES, cb), lambda g, c: (g, 0, c)),
        ],
        out_specs=out_specs,
        out_shape=out_shape,
        scratch_shapes=[pltpu.VMEM((SEG + RG_CONV - 1, SUBLANES, cb), F32)]
        + [pltpu.VMEM((chunk_len, SUBLANES, cb), F32) for _ in range(8)],
        compiler_params=_params("arbitrary", "arbitrary"),
        name="rglru_branch",
    )(p, conv_w, conv_b, gate_w, gate_b, lam, h0f, h0b)


def _merge_kernel(hrg_ref, gate_ref, scb_ref, scc_ref, scx_ref, grg_ref, gsc_ref, scw_ref, bm_ref,
                  wrg_ref, wsc_ref, o_ref, a_scr, s_scr, q_scr, acc_rg, acc_sc, *, seq_chunks, wrap, nb):
    s = pl.program_id(0)
    last = pl.num_programs(0) - 1
    c_prev = (s + nb - 1) % nb
    tc = a_scr.shape[2]

    def produce(slot):
        a_scr[slot] = (hrg_ref[...].astype(F32) * _gelu_tanh(gate_ref[...].astype(F32))).astype(BF16)
        q = scc_ref[...].astype(F32) * scx_ref[...].astype(F32)
        _fill_halo(q_scr, q.reshape(SEG, SUBLANES, tc), SC_PAD_LO, SC_CONV - 1 - SC_PAD_LO,
                   seq_chunks if wrap else None)
        cq = _conv_from_halo(q_scr, scw_ref[...], SEG).reshape(ROW_TILE, tc)
        s_scr[slot] = (scb_ref[...].astype(F32) * cq).astype(BF16)

    def consume(slot):
        keep = c_prev != 0
        krows = pl.ds(pl.multiple_of(c_prev * tc, tc), tc)
        acc_rg[...] = (jnp.where(keep, acc_rg[...], 0.0)
                       + jnp.dot(a_scr[slot], wrg_ref[krows, :], preferred_element_type=F32))
        acc_sc[...] = (jnp.where(keep, acc_sc[...], 0.0)
                       + jnp.dot(s_scr[slot], wsc_ref[krows, :], preferred_element_type=F32))

    @pl.when(s == 0)
    def _():
        acc_rg[...] = jnp.zeros_like(acc_rg)
        acc_sc[...] = jnp.zeros_like(acc_sc)
        produce(0)

    @pl.when(jnp.logical_and(s > 0, s < last))
    def _():
        produce(s % 2)
        consume((s + 1) % 2)

    @pl.when(s == last)
    def _():
        consume((s + 1) % 2)

    @pl.when(jnp.logical_and(s > 0, c_prev == nb - 1))
    def _():
        bm = bm_ref[...]
        m = (_sigmoid_tanh(grg_ref[...].astype(F32) + bm[0:1]) * acc_rg[...]
             + _sigmoid_tanh(gsc_ref[...].astype(F32) + bm[1:2]) * acc_sc[...])
        o_ref[...] = m.astype(o_ref.dtype)


def _mixer_merge(p, h_rg, sc_conv_w, b_merge, w_rg_out, w_sc_out, *, layer, seq_chunks, wrap):
    t, d = h_rg.shape
    tm, tc = ROW_TILE, 512
    nb = d // tc
    n_steps = (t // tm) * nb
    built = lambda s: jnp.minimum(s, n_steps - 1)
    used = lambda s: jnp.maximum(s - 1, 0)
    chunk = lambda sec: pl.BlockSpec((tm, tc), lambda s: (built(s) // nb, sec * nb + built(s) % nb))
    resident = pl.BlockSpec((None, d, d), lambda s: (layer, 0, 0), pipeline_mode=pl.Buffered(1))
    return pl.pallas_call(
        functools.partial(_merge_kernel, seq_chunks=seq_chunks, wrap=wrap, nb=nb),
        grid=(n_steps + 1,),
        in_specs=[
            chunk(0), chunk(1), chunk(2), chunk(3), chunk(4),
            pl.BlockSpec((tm, d), lambda s: (used(s) // nb, 5)),
            pl.BlockSpec((tm, d), lambda s: (used(s) // nb, 6)),
            pl.BlockSpec((None, SC_CONV, tc), lambda s: (layer, 0, built(s) % nb)),
            pl.BlockSpec((None, 2, d), lambda s: (layer, 0, 0)),
            resident, resident,
        ],
        out_specs=pl.BlockSpec((tm, d), lambda s: (used(s) // nb, 0)),
        out_shape=jax.ShapeDtypeStruct((t, d), BF16),
        scratch_shapes=[pltpu.VMEM((2, tm, tc), BF16), pltpu.VMEM((2, tm, tc), BF16),
                        pltpu.VMEM((SEG + SC_CONV - 1, SUBLANES, tc), F32),
                        pltpu.VMEM((tm, d), F32), pltpu.VMEM((tm, d), F32)],
        compiler_params=_params("arbitrary"),
        name="mixer_merge",
    )(h_rg, p, p, p, p, p, p, sc_conv_w, b_merge, w_rg_out, w_sc_out)


def _oproj_kernel(m_ref, x_ref, mod_ref, wo_ref, lng_ref, lnb_ref, o_ref, *, alpha):
    y = jnp.dot(m_ref[...], wo_ref[...], preferred_element_type=F32)
    z = alpha * x_ref[...] + mod_ref[...][5:6] * y
    o_ref[...] = _layer_norm(z, lng_ref[...], lnb_ref[...])


def _mixer_oproj(m, x, mods, mod_row, w_o, ln_g, ln_b, *, layer, ln_idx, alpha):
    t, d = x.shape
    tm = ROW_TILE
    return pl.pallas_call(
        functools.partial(_oproj_kernel, alpha=alpha),
        grid=(t // tm,),
        in_specs=[
            pl.BlockSpec((tm, d), lambda i: (i, 0)),
            pl.BlockSpec((tm, d), lambda i: (i, 0)),
            pl.BlockSpec((None, None, N_MOD, d), lambda i: (layer, mod_row(i * tm), 0, 0)),
            pl.BlockSpec((None, d, d), lambda i: (layer, 0, 0)),
            pl.BlockSpec((None, 1, d), lambda i: (ln_idx, 0, 0)),
            pl.BlockSpec((None, 1, d), lambda i: (ln_idx, 0, 0)),
        ],
        out_specs=pl.BlockSpec((tm, d), lambda i: (i, 0)),
        out_shape=jax.ShapeDtypeStruct((t, d), F32),
        compiler_params=_params("arbitrary"),
        name="mixer_oproj",
    )(m, x, mods, w_o, ln_g, ln_b)


def kernel(x, c, ctx, c_ctx, w_mod, b_mod, ln_g, ln_b, ffn1_w_in, ffn1_w_out, ffn2_w_in, ffn2_w_out, w_in, rg_conv_w, rg_conv_b, rg_gate_w, rg_gate_b, rg_lam, sc_conv_w, w_rg_out, w_sc_out, b_merge, w_o):
    batch, seq, d = x.shape
    ctx_len = ctx.shape[1]
    depth = w_mod.shape[0]
    w_rg = rg_conv_w.shape[2]
    n_ln = ln_g.shape[1]
    alpha = (2 * depth) ** 0.25
    lat_chunk = seq // LAT_CHUNKS
    ctx_chunk = ctx_len // CTX_CHUNKS
    ctx_group = SUBLANES // CTX_CHUNKS
    assert lat_chunk % GRID_W == 0 and ctx_chunk == SEG and batch % ctx_group == 0
    assert w_rg == RG_BLOCKS * LANES and d % LANES == 0

    xl = x.reshape(batch, LAT_CHUNKS, lat_chunk, d).transpose(0, 2, 1, 3).reshape(batch * seq, d)
    xc = ctx.reshape(batch // ctx_group, ctx_group, CTX_CHUNKS, ctx_chunk, d).transpose(0, 3, 1, 2, 4)
    xc = xc.reshape(batch * ctx_len, d)

    cond = jnp.zeros((SUBLANES, d), F32).at[:batch].set(c).at[batch].set(c_ctx)
    mods = _mod_vectors(cond, w_mod, b_mod).reshape(depth, SUBLANES, N_MOD, d)
    lat_row = lambda r: r // seq
    ctx_row = lambda r: batch

    bf = lambda a: a.astype(BF16)
    f1_in, f1_out, f2_in, f2_out = bf(ffn1_w_in), bf(ffn1_w_out), bf(ffn2_w_in), bf(ffn2_w_out)
    w_in_b, w_rg_out_b, w_sc_out_b, w_o_b = bf(w_in), bf(w_rg_out), bf(w_sc_out), bf(w_o)
    ln_g = ln_g.reshape(depth * n_ln, 1, d)
    ln_b = ln_b.reshape(depth * n_ln, 1, d)
    gw = bf(rg_gate_w.transpose(0, 3, 4, 1, 2, 5).reshape(depth, RG_BLOCKS, LANES, 4 * LANES))
    gb = rg_gate_b.reshape(depth, 4, RG_BLOCKS, LANES).transpose(0, 2, 1, 3).reshape(depth, RG_BLOCKS, 1, 4 * LANES)
    rg_args = (rg_conv_w, rg_conv_b.reshape(depth, 1, w_rg), gw, gb, rg_lam)
    zeros_state = jnp.zeros((batch // ctx_group, SUBLANES, w_rg), F32)

    for l in range(depth):
        last = l == depth - 1
        ffn = functools.partial(_ffn_sublayer, layer=l, alpha=alpha)
        ln = lambda k: dict(ln_g=ln_g, ln_b=ln_b, ln_idx=l * n_ln + k)

        xl, hl = ffn(xl, mods, lat_row, f1_in, f1_out, **ln(0), k0=0, emit_h=True)
        xc, hc = ffn(xc, mods, ctx_row, f1_in, f1_out, **ln(0), k0=0, emit_h=True)

        pc = _project(hc, w_in_b, w_rg if last else N_PROJ * d, layer=l)
        hrg_c, s_f, s_b = _rglru_branch(pc, *rg_args, zeros_state, zeros_state, layer=l, chunk_len=ctx_chunk,
                                        seq_chunks=CTX_CHUNKS, wrap=True, emit_state=True)
        if not last:
            mc = _mixer_merge(pc, hrg_c, sc_conv_w, b_merge, w_rg_out_b, w_sc_out_b, layer=l,
                              seq_chunks=CTX_CHUNKS, wrap=True)
            xc = _mixer_oproj(mc, xc, mods, ctx_row, w_o_b, **ln(1), layer=l, alpha=alpha)
        s_f = s_f.reshape(batch, CTX_CHUNKS, w_rg)[:, CTX_CHUNKS - 1]
        s_b = s_b.reshape(batch, CTX_CHUNKS, w_rg)[:, 0]
        h0f = jnp.broadcast_to(s_f[:, None, :], (batch, SUBLANES, w_rg))
        h0b = jnp.broadcast_to(s_b[:, None, :], (batch, SUBLANES, w_rg))

        pl_ = _project(hl, w_in_b, N_PROJ * d, layer=l)
        (hrg_l,) = _rglru_branch(pl_, *rg_args, h0f, h0b, layer=l, chunk_len=lat_chunk, seq_chunks=LAT_CHUNKS,
                                 wrap=False, emit_state=False)
        ml = _mixer_merge(pl_, hrg_l, sc_conv_w, b_merge, w_rg_out_b, w_sc_out_b, layer=l,
                          seq_chunks=LAT_CHUNKS, wrap=False)
        xl = _mixer_oproj(ml, xl, mods, lat_row, w_o_b, **ln(1), layer=l, alpha=alpha)

        xl = ffn(xl, mods, lat_row, f2_in, f2_out, **ln(2), k0=6, emit_h=False)
        if not last:
            xc = ffn(xc, mods, ctx_row, f2_in, f2_out, **ln(2), k0=6, emit_h=False)

    return xl.reshape(batch, lat_chunk, LAT_CHUNKS, d).transpose(0, 2, 1, 3).reshape(batch, seq, d)
```

```python
import functools
import math

import jax
import jax.numpy as jnp
from jax import lax
from jax.experimental import pallas as pl
from jax.experimental.pallas import tpu as pltpu

F32 = jnp.float32
BF16 = jnp.bfloat16

SUBLANES = 8
LANES = 128
VMEM_BYTES_V7X = 64 * 1024 * 1024
VMEM_LIMIT = VMEM_BYTES_V7X - 8 * 1024 * 1024

GRID_W = 64
N_MOD = 9
RG_BLOCKS = 16
RG_CONV = 4
RG_PAD_LO = 2
SC_CONV = 3
SC_PAD_LO = 1
RG_C = 8.0
LN_EPS = 1e-5
N_PROJ = 7

SEG = GRID_W
ROW_TILE = SEG * SUBLANES
LAT_CHUNKS = SUBLANES
CTX_CHUNKS = 4
SCAN_BLOCK = 8
LN_ROWS = 128
FFN_ROW_TILE = 2 * ROW_TILE
MM_ROWS = 256


def _sigmoid(x):
    return 1.0 / (1.0 + jnp.exp(-x))


def _sigmoid_tanh(x):
    return 0.5 + 0.5 * jnp.tanh(0.5 * x)


def _gelu_tanh(x):
    return x * (0.5 * (1.0 + jnp.tanh(math.sqrt(2.0 / math.pi) * (x + 0.044715 * (x * x * x)))))


def _layer_norm(z, g, b):
    mu = jnp.mean(z, axis=-1, keepdims=True)
    zc = z - mu
    var = jnp.mean(zc * zc, axis=-1, keepdims=True)
    return zc * lax.rsqrt(var + LN_EPS) * g + b


def _params(*sem):
    return pltpu.CompilerParams(dimension_semantics=sem, vmem_limit_bytes=VMEM_LIMIT)


def _mod_kernel(c_ref, w_ref, b_ref, o_ref):
    @pl.when(pl.program_id(1) == 0)
    def _():
        o_ref[...] = jnp.broadcast_to(b_ref[...], o_ref.shape)

    c = c_ref[...]
    s = (c * _sigmoid(c)).astype(BF16)
    o_ref[...] += jnp.dot(s, w_ref[...].astype(BF16), preferred_element_type=F32)


def _mod_vectors(cond, w_mod, b_mod):
    depth, d, n = w_mod.shape
    tk = LANES
    return pl.pallas_call(
        _mod_kernel,
        grid=(depth, d // tk),
        in_specs=[
            pl.BlockSpec((SUBLANES, tk), lambda l, k: (0, k)),
            pl.BlockSpec((None, tk, n), lambda l, k: (l, k, 0)),
            pl.BlockSpec((None, 1, n), lambda l, k: (l, 0, 0)),
        ],
        out_specs=pl.BlockSpec((None, SUBLANES, n), lambda l, k: (l, 0, 0)),
        out_shape=jax.ShapeDtypeStruct((depth, SUBLANES, n), F32),
        compiler_params=_params("arbitrary", "arbitrary"),
        name="mod_vectors",
    )(cond, w_mod, b_mod.reshape(depth, 1, n))


def _ffn_kernel(x_ref, mod_ref, wg_ref, wu_ref, wo_ref, lng_ref, lnb_ref, *rest, k0, alpha, emit_h):
    if emit_h:
        o_ref, h2_ref = rest
    else:
        (o_ref,) = rest
    f = pl.program_id(1)
    m = mod_ref[...]
    scale1 = 1.0 + m[k0 + 1:k0 + 2]
    shift = m[k0:k0 + 1]

    def hidden(rows):
        h = (x_ref[rows, :] * scale1 + shift).astype(BF16)
        g = jnp.dot(h, wg_ref[...], preferred_element_type=F32)
        u = jnp.dot(h, wu_ref[...], preferred_element_type=F32)
        a = (g * _sigmoid(g) * u).astype(BF16)
        return jnp.dot(a, wo_ref[...], preferred_element_type=F32)

    row_blocks = [slice(r0, r0 + MM_ROWS) for r0 in range(0, x_ref.shape[0], MM_ROWS)]

    @pl.when(f == 0)
    def _():
        for rows in row_blocks:
            o_ref[rows, :] = hidden(rows)

    @pl.when(f > 0)
    def _():
        for rows in row_blocks:
            o_ref[rows, :] += hidden(rows)

    @pl.when(f == pl.num_programs(1) - 1)
    def _():
        half_gate = 0.5 * m[k0 + 2:k0 + 3]
        g_ln = lng_ref[...]
        b_ln = lnb_ref[...]
        if emit_h:
            scale2 = 1.0 + m[k0 + 4:k0 + 5]
            g2 = g_ln * scale2
            b2 = b_ln * scale2 + m[k0 + 3:k0 + 4]

        def finish_rows(sb, _):
            rows = pl.ds(pl.multiple_of(sb * LN_ROWS, LN_ROWS), LN_ROWS)
            z = alpha * x_ref[rows, :] + half_gate * o_ref[rows, :]
            mu = jnp.mean(z, axis=-1, keepdims=True)
            zc = z - mu
            var = jnp.mean(zc * zc, axis=-1, keepdims=True)
            zn = zc * lax.rsqrt(var + LN_EPS)
            o_ref[rows, :] = zn * g_ln + b_ln
            if emit_h:
                h2_ref[rows, :] = (zn * g2 + b2).astype(BF16)
            return 0

        lax.fori_loop(0, x_ref.shape[0] // LN_ROWS, finish_rows, 0)


def _ffn_sublayer(x, mods, mod_row, w_in, w_out, ln_g, ln_b, *, layer, ln_idx, k0, alpha, emit_h):
    t, d = x.shape
    f_dim = w_out.shape[1]
    tm, tf = min(FFN_ROW_TILE, t), 512
    nf = f_dim // tf
    out_shape = [jax.ShapeDtypeStruct((t, d), F32)]
    out_specs = [pl.BlockSpec((tm, d), lambda i, f: (i, 0))]
    if emit_h:
        out_shape.append(jax.ShapeDtypeStruct((t, d), BF16))
        out_specs.append(pl.BlockSpec((tm, d), lambda i, f: (i, 0)))
    res = pl.pallas_call(
        functools.partial(_ffn_kernel, k0=k0, alpha=alpha, emit_h=emit_h),
        grid=(t // tm, nf),
        in_specs=[
            pl.BlockSpec((tm, d), lambda i, f: (i, 0)),
            pl.BlockSpec((None, None, N_MOD, d), lambda i, f: (layer, mod_row(i * tm), 0, 0)),
            pl.BlockSpec((None, d, tf), lambda i, f: (layer, 0, f)),
            pl.BlockSpec((None, d, tf), lambda i, f: (layer, 0, f + nf)),
            pl.BlockSpec((None, tf, d), lambda i, f: (layer, f, 0)),
            pl.BlockSpec((None, 1, d), lambda i, f: (ln_idx, 0, 0)),
            pl.BlockSpec((None, 1, d), lambda i, f: (ln_idx, 0, 0)),
        ],
        out_specs=out_specs,
        out_shape=out_shape,
        compiler_params=_params("arbitrary", "arbitrary"),
        name="ffn_sublayer",
    )(x, mods, w_in, w_in, w_out, ln_g, ln_b)
    return res if emit_h else res[0]


def _matmul_kernel(a_ref, b_ref, o_ref):
    o_ref[...] = jnp.dot(a_ref[...], b_ref[...], preferred_element_type=F32).astype(o_ref.dtype)


def _project(h, w, n_cols, *, layer):
    t, d = h.shape
    tm = min(t, 1024)
    tn = 2048
    return pl.pallas_call(
        _matmul_kernel,
        grid=(t // tm, n_cols // tn),
        in_specs=[
            pl.BlockSpec((tm, d), lambda i, j: (i, 0)),
            pl.BlockSpec((None, d, tn), lambda i, j: (layer, 0, j)),
        ],
        out_specs=pl.BlockSpec((tm, tn), lambda i, j: (i, j)),
        out_shape=jax.ShapeDtypeStruct((t, n_cols), BF16),
        compiler_params=_params("arbitrary", "arbitrary"),
        name="in_proj",
    )(h, w)


def _fill_halo(scr, x3, lo, hi, seq_chunks):
    n = x3.shape[0]
    scr[lo:lo + n] = x3
    tile = x3.shape[1:]
    if seq_chunks is None:
        if lo:
            scr[0:lo] = jnp.zeros((lo,) + tile, F32)
        if hi:
            scr[lo + n:lo + n + hi] = jnp.zeros((hi,) + tile, F32)
        return
    chunk = lax.broadcasted_iota(jnp.int32, tile, 0) % seq_chunks
    for t in range(lo):
        prev = pltpu.roll(x3[n - lo + t], 1, 0)
        scr[t] = jnp.where(chunk != 0, prev, 0.0)
    for t in range(hi):
        nxt = pltpu.roll(x3[t], SUBLANES - 1, 0)
        scr[lo + n + t] = jnp.where(chunk != seq_chunks - 1, nxt, 0.0)


def _conv_from_halo(scr, w, n):
    out = w[0:1] * scr[0:n]
    for k in range(1, w.shape[0]):
        out = out + w[k:k + 1] * scr[k:k + n]
    return out


def _rglru_kernel(p_ref, cw_ref, cb_ref, gw_ref, gb_ref, lam_ref, h0f_ref, h0b_ref, *rest,
                  chunk_len, seq_chunks, wrap, emit_state):
    if emit_state:
        h_ref, sf_ref, sb_ref, x_scr, a_f, v_f, a_b, v_b, h_f, p_f, h_b, p_b = rest
    else:
        h_ref, x_scr, a_f, v_f, a_b, v_b, h_f, p_f, h_b, p_b = rest
    n_seg = chunk_len // SEG
    cw = cw_ref[...]
    cb = cb_ref[...]
    lam = lam_ref[...]
    log_sig = jnp.minimum(lam, 0.0) - jnp.log1p(jnp.exp(-jnp.abs(lam)))
    e_fac = (0.5 * RG_C / math.log(2.0)) * log_sig
    gw_half = gw_ref[...] * 0.5
    gb_half = gb_ref[...] * 0.5
    a_scr = (a_f, a_b)
    v_scr = (v_f, v_b)

    def gates(si, _):
        r0 = pl.multiple_of(si * ROW_TILE, ROW_TILE)
        x3 = p_ref[pl.ds(r0, ROW_TILE), :].astype(F32).reshape(SEG, SUBLANES, LANES)
        _fill_halo(x_scr, x3, RG_PAD_LO, RG_CONV - 1 - RG_PAD_LO, seq_chunks if wrap else None)
        xc = (_conv_from_halo(x_scr, cw, SEG) + cb).reshape(ROW_TILE, LANES)
        t = jnp.tanh(jnp.dot(xc.astype(BF16), gw_half, preferred_element_type=F32) + gb_half)
        xh = 0.5 * xc
        j0 = pl.multiple_of(si * SEG, SEG)
        for d in range(2):
            t_r = t[:, (2 * d) * LANES:(2 * d + 1) * LANES]
            t_i = t[:, (2 * d + 1) * LANES:(2 * d + 2) * LANES]
            ef = e_fac[d:d + 1]
            a = jnp.exp2(ef * t_r + ef)
            y = 1.0 - a * a
            s = jnp.where(y > 0.0, y * lax.rsqrt(y), 0.0)
            v = s * (xh * t_i + xh)
            a_scr[d][pl.ds(j0, SEG)] = a.reshape(SEG, SUBLANES, LANES)
            v_scr[d][pl.ds(j0, SEG)] = v.reshape(SEG, SUBLANES, LANES)
        return 0

    lax.fori_loop(0, n_seg, gates, 0)

    def steps(blk, carry):
        hf, pf, hb, pb = carry
        jf = pl.ds(pl.multiple_of(blk * SCAN_BLOCK, SCAN_BLOCK), SCAN_BLOCK)
        jb = pl.ds(pl.multiple_of(chunk_len - SCAN_BLOCK - blk * SCAN_BLOCK, SCAN_BLOCK), SCAN_BLOCK)
        a, v = a_f[jf], v_f[jf]
        hs, ps = [], []
        for u in range(SCAN_BLOCK):
            hf = a[u] * hf + v[u]
            pf = a[u] * pf
            hs.append(hf)
            ps.append(pf)
        h_f[jf] = jnp.stack(hs)
        p_f[jf] = jnp.stack(ps)
        a, v = a_b[jb], v_b[jb]
        hs, ps = [], []
        for u in reversed(range(SCAN_BLOCK)):
            hb = a[u] * hb + v[u]
            pb = a[u] * pb
            hs.append(hb)
            ps.append(pb)
        h_b[jb] = jnp.stack(hs[::-1])
        p_b[jb] = jnp.stack(ps[::-1])
        return hf, pf, hb, pb

    zero = jnp.zeros((SUBLANES, LANES), F32)
    one = jnp.ones((SUBLANES, LANES), F32)
    hf, pf, hb, pb = lax.fori_loop(0, chunk_len // SCAN_BLOCK, steps, (zero, one, zero, one))

    sub = lax.broadcasted_iota(jnp.int32, (SUBLANES, LANES), 0)
    h0f = h0f_ref[...]
    h0b = h0b_ref[...]
    cf = jnp.zeros((SUBLANES, LANES), F32)
    row = None
    for s in range(SUBLANES):
        if s % seq_chunks == 0:
            row = h0f[s:s + 1]
        else:
            row = hf[s - 1:s] + pf[s - 1:s] * row
        cf = jnp.where(sub == s, row, cf)
    cbk = jnp.zeros((SUBLANES, LANES), F32)
    for s in reversed(range(SUBLANES)):
        if s % seq_chunks == seq_chunks - 1:
            row = h0b[s:s + 1]
        else:
            row = hb[s + 1:s + 2] + pb[s + 1:s + 2] * row
        cbk = jnp.where(sub == s, row, cbk)

    if emit_state:
        sf_ref[...] = hf + pf * cf
        sb_ref[...] = hb + pb * cbk

    def combine(si, _):
        j0 = pl.multiple_of(si * SEG, SEG)
        r0 = pl.multiple_of(si * ROW_TILE, ROW_TILE)
        js = pl.ds(j0, SEG)
        h = (h_f[js] + p_f[js] * cf) + (h_b[js] + p_b[js] * cbk)
        h_ref[pl.ds(r0, ROW_TILE), :] = h.reshape(ROW_TILE, LANES).astype(h_ref.dtype)
        return 0

    lax.fori_loop(0, n_seg, combine, 0)


def _rglru_branch(p, conv_w, conv_b, gate_w, gate_b, lam, h0f, h0b, *, layer, chunk_len, seq_chunks, wrap,
                  emit_state):
    t = p.shape[0]
    w = conv_w.shape[2]
    rows = chunk_len * SUBLANES
    groups = t // rows
    cb = LANES
    out_shape = [jax.ShapeDtypeStruct((t, w), BF16)]
    out_specs = [pl.BlockSpec((rows, cb), lambda g, c: (g, c))]
    if emit_state:
        for _ in range(2):
            out_shape.append(jax.ShapeDtypeStruct((groups, SUBLANES, w), F32))
            out_specs.append(pl.BlockSpec((None, SUBLANES, cb), lambda g, c: (g, 0, c)))
    return pl.pallas_call(
        functools.partial(_rglru_kernel, chunk_len=chunk_len, seq_chunks=seq_chunks, wrap=wrap,
                          emit_state=emit_state),
        grid=(groups, w // cb),
        in_specs=[
            pl.BlockSpec((rows, cb), lambda g, c: (g, c)),
            pl.BlockSpec((None, RG_CONV, cb), lambda g, c: (layer, 0, c)),
            pl.BlockSpec((None, 1, cb), lambda g, c: (layer, 0, c)),
            pl.BlockSpec((None, None, cb, 4 * cb), lambda g, c: (layer, c, 0, 0)),
            pl.BlockSpec((None, None, 1, 4 * cb), lambda g, c: (layer, c, 0, 0)),
            pl.BlockSpec((None, 2, cb), lambda g, c: (layer, 0, c)),
            pl.BlockSpec((None, SUBLANES, cb), lambda g, c: (g, 0, c)),
            pl.BlockSpec((None, SUBLANES, cb), lambda g, c: (g, 0, c)),
        ],
        out_specs=out_specs,
        out_shape=out_shape,
        scratch_shapes=[pltpu.VMEM((SEG + RG_CONV - 1, SUBLANES, cb), F32)]
        + [pltpu.VMEM((chunk_len, SUBLANES, cb), F32) for _ in range(8)],
        compiler_params=_params("arbitrary", "arbitrary"),
        name="rglru_branch",
    )(p, conv_w, conv_b, gate_w, gate_b, lam, h0f, h0b)


def _merge_kernel(hrg_ref, gate_ref, scb_ref, scc_ref, scx_ref, grg_ref, gsc_ref, scw_ref, bm_ref,
                  wrg_ref, wsc_ref, o_ref, a_scr, s_scr, q_scr, acc_rg, acc_sc, *, seq_chunks, wrap, nb):
    s = pl.program_id(0)
    last = pl.num_programs(0) - 1
    c_prev = (s + nb - 1) % nb
    tc = a_scr.shape[2]

    def produce(slot):
        a_scr[slot] = (hrg_ref[...].astype(F32) * _gelu_tanh(gate_ref[...].astype(F32))).astype(BF16)
        q = scc_ref[...].astype(F32) * scx_ref[...].astype(F32)
        _fill_halo(q_scr, q.reshape(SEG, SUBLANES, tc), SC_PAD_LO, SC_CONV - 1 - SC_PAD_LO,
                   seq_chunks if wrap else None)
        cq = _conv_from_halo(q_scr, scw_ref[...], SEG).reshape(ROW_TILE, tc)
        s_scr[slot] = (scb_ref[...].astype(F32) * cq).astype(BF16)

    def consume(slot):
        keep = c_prev != 0
        krows = pl.ds(pl.multiple_of(c_prev * tc, tc), tc)
        acc_rg[...] = (jnp.where(keep, acc_rg[...], 0.0)
                       + jnp.dot(a_scr[slot], wrg_ref[krows, :], preferred_element_type=F32))
        acc_sc[...] = (jnp.where(keep, acc_sc[...], 0.0)
                       + jnp.dot(s_scr[slot], wsc_ref[krows, :], preferred_element_type=F32))

    @pl.when(s == 0)
    def _():
        acc_rg[...] = jnp.zeros_like(acc_rg)
        acc_sc[...] = jnp.zeros_like(acc_sc)
        produce(0)

    @pl.when(jnp.logical_and(s > 0, s < last))
    def _():
        produce(s % 2)
        consume((s + 1) % 2)

    @pl.when(s == last)
    def _():
        consume((s + 1) % 2)

    @pl.when(jnp.logical_and(s > 0, c_prev == nb - 1))
    def _():
        bm = bm_ref[...]
        m = (_sigmoid_tanh(grg_ref[...].astype(F32) + bm[0:1]) * acc_rg[...]
             + _sigmoid_tanh(gsc_ref[...].astype(F32) + bm[1:2]) * acc_sc[...])
        o_ref[...] = m.astype(o_ref.dtype)


def _mixer_merge(p, h_rg, sc_conv_w, b_merge, w_rg_out, w_sc_out, *, layer, seq_chunks, wrap):
    t, d = h_rg.shape
    tm, tc = ROW_TILE, 512
    nb = d // tc
    n_steps = (t // tm) * nb
    built = lambda s: jnp.minimum(s, n_steps - 1)
    used = lambda s: jnp.maximum(s - 1, 0)
    chunk = lambda sec: pl.BlockSpec((tm, tc), lambda s: (built(s) // nb, sec * nb + built(s) % nb))
    resident = pl.BlockSpec((None, d, d), lambda s: (layer, 0, 0), pipeline_mode=pl.Buffered(1))
    return pl.pallas_call(
        functools.partial(_merge_kernel, seq_chunks=seq_chunks, wrap=wrap, nb=nb),
        grid=(n_steps + 1,),
        in_specs=[
            chunk(0), chunk(1), chunk(2), chunk(3), chunk(4),
            pl.BlockSpec((tm, d), lambda s: (used(s) // nb, 5)),
            pl.BlockSpec((tm, d), lambda s: (used(s) // nb, 6)),
            pl.BlockSpec((None, SC_CONV, tc), lambda s: (layer, 0, built(s) % nb)),
            pl.BlockSpec((None, 2, d), lambda s: (layer, 0, 0)),
            resident, resident,
        ],
        out_specs=pl.BlockSpec((tm, d), lambda s: (used(s) // nb, 0)),
        out_shape=jax.ShapeDtypeStruct((t, d), BF16),
        scratch_shapes=[pltpu.VMEM((2, tm, tc), BF16), pltpu.VMEM((2, tm, tc), BF16),
                        pltpu.VMEM((SEG + SC_CONV - 1, SUBLANES, tc), F32),
                        pltpu.VMEM((tm, d), F32), pltpu.VMEM((tm, d), F32)],
        compiler_params=_params("arbitrary"),
        name="mixer_merge",
    )(h_rg, p, p, p, p, p, p, sc_conv_w, b_merge, w_rg_out, w_sc_out)


def _oproj_kernel(m_ref, x_ref, mod_ref, wo_ref, lng_ref, lnb_ref, o_ref, *, alpha):
    gate = mod_ref[...][5:6]
    wo = wo_ref[...]
    g_ln = lng_ref[...]
    b_ln = lnb_ref[...]
    for r0 in range(0, m_ref.shape[0], LN_ROWS):
        rows = slice(r0, r0 + LN_ROWS)
        y = jnp.dot(m_ref[rows, :], wo, preferred_element_type=F32)
        z = alpha * x_ref[rows, :] + gate * y
        o_ref[rows, :] = _layer_norm(z, g_ln, b_ln)


def _mixer_oproj(m, x, mods, mod_row, w_o, ln_g, ln_b, *, layer, ln_idx, alpha):
    t, d = x.shape
    tm = ROW_TILE
    return pl.pallas_call(
        functools.partial(_oproj_kernel, alpha=alpha),
        grid=(t // tm,),
        in_specs=[
            pl.BlockSpec((tm, d), lambda i: (i, 0)),
            pl.BlockSpec((tm, d), lambda i: (i, 0)),
            pl.BlockSpec((None, None, N_MOD, d), lambda i: (layer, mod_row(i * tm), 0, 0)),
            pl.BlockSpec((None, d, d), lambda i: (layer, 0, 0)),
            pl.BlockSpec((None, 1, d), lambda i: (ln_idx, 0, 0)),
            pl.BlockSpec((None, 1, d), lambda i: (ln_idx, 0, 0)),
        ],
        out_specs=pl.BlockSpec((tm, d), lambda i: (i, 0)),
        out_shape=jax.ShapeDtypeStruct((t, d), F32),
        compiler_params=_params("arbitrary"),
        name="mixer_oproj",
    )(m, x, mods, w_o, ln_g, ln_b)


def kernel(x, c, ctx, c_ctx, w_mod, b_mod, ln_g, ln_b, ffn1_w_in, ffn1_w_out, ffn2_w_in, ffn2_w_out, w_in, rg_conv_w, rg_conv_b, rg_gate_w, rg_gate_b, rg_lam, sc_conv_w, w_rg_out, w_sc_out, b_merge, w_o):
    batch, seq, d = x.shape
    ctx_len = ctx.shape[1]
    depth = w_mod.shape[0]
    w_rg = rg_conv_w.shape[2]
    n_ln = ln_g.shape[1]
    alpha = (2 * depth) ** 0.25
    lat_chunk = seq // LAT_CHUNKS
    ctx_chunk = ctx_len // CTX_CHUNKS
    ctx_group = SUBLANES // CTX_CHUNKS
    assert lat_chunk % GRID_W == 0 and ctx_chunk == SEG and batch % ctx_group == 0
    assert w_rg == RG_BLOCKS * LANES and d % LANES == 0

    xl = x.reshape(batch, LAT_CHUNKS, lat_chunk, d).transpose(0, 2, 1, 3).reshape(batch * seq, d)
    xc = ctx.reshape(batch // ctx_group, ctx_group, CTX_CHUNKS, ctx_chunk, d).transpose(0, 3, 1, 2, 4)
    xc = xc.reshape(batch * ctx_len, d)

    cond = jnp.zeros((SUBLANES, d), F32).at[:batch].set(c).at[batch].set(c_ctx)
    mods = _mod_vectors(cond, w_mod, b_mod).reshape(depth, SUBLANES, N_MOD, d)
    lat_row = lambda r: r // seq
    ctx_row = lambda r: batch

    bf = lambda a: a.astype(BF16)
    f1_in, f1_out, f2_in, f2_out = bf(ffn1_w_in), bf(ffn1_w_out), bf(ffn2_w_in), bf(ffn2_w_out)
    w_in_b, w_rg_out_b, w_sc_out_b, w_o_b = bf(w_in), bf(w_rg_out), bf(w_sc_out), bf(w_o)
    ln_g = ln_g.reshape(depth * n_ln, 1, d)
    ln_b = ln_b.reshape(depth * n_ln, 1, d)
    gw = bf(rg_gate_w.transpose(0, 3, 4, 1, 2, 5).reshape(depth, RG_BLOCKS, LANES, 4 * LANES))
    gb = rg_gate_b.reshape(depth, 4, RG_BLOCKS, LANES).transpose(0, 2, 1, 3).reshape(depth, RG_BLOCKS, 1, 4 * LANES)
    rg_args = (rg_conv_w, rg_conv_b.reshape(depth, 1, w_rg), gw, gb, rg_lam)
    zeros_state = jnp.zeros((batch // ctx_group, SUBLANES, w_rg), F32)

    for l in range(depth):
        last = l == depth - 1
        ffn = functools.partial(_ffn_sublayer, layer=l, alpha=alpha)
        ln = lambda k: dict(ln_g=ln_g, ln_b=ln_b, ln_idx=l * n_ln + k)

        xl, hl = ffn(xl, mods, lat_row, f1_in, f1_out, **ln(0), k0=0, emit_h=True)
        xc, hc = ffn(xc, mods, ctx_row, f1_in, f1_out, **ln(0), k0=0, emit_h=True)

        pc = _project(hc, w_in_b, w_rg if last else N_PROJ * d, layer=l)
        hrg_c, s_f, s_b = _rglru_branch(pc, *rg_args, zeros_state, zeros_state, layer=l, chunk_len=ctx_chunk,
                                        seq_chunks=CTX_CHUNKS, wrap=True, emit_state=True)
        if not last:
            mc = _mixer_merge(pc, hrg_c, sc_conv_w, b_merge, w_rg_out_b, w_sc_out_b, layer=l,
                              seq_chunks=CTX_CHUNKS, wrap=True)
            xc = _mixer_oproj(mc, xc, mods, ctx_row, w_o_b, **ln(1), layer=l, alpha=alpha)
        s_f = s_f.reshape(batch, CTX_CHUNKS, w_rg)[:, CTX_CHUNKS - 1]
        s_b = s_b.reshape(batch, CTX_CHUNKS, w_rg)[:, 0]
        h0f = jnp.broadcast_to(s_f[:, None, :], (batch, SUBLANES, w_rg))
        h0b = jnp.broadcast_to(s_b[:, None, :], (batch, SUBLANES, w_rg))

        pl_ = _project(hl, w_in_b, N_PROJ * d, layer=l)
        (hrg_l,) = _rglru_branch(pl_, *rg_args, h0f, h0b, layer=l, chunk_len=lat_chunk, seq_chunks=LAT_CHUNKS,
                                 wrap=False, emit_state=False)
        ml = _mixer_merge(pl_, hrg_l, sc_conv_w, b_merge, w_rg_out_b, w_sc_out_b, layer=l,
                          seq_chunks=LAT_CHUNKS, wrap=False)
        xl = _mixer_oproj(ml, xl, mods, lat_row, w_o_b, **ln(1), layer=l, alpha=alpha)

        xl = ffn(xl, mods, lat_row, f2_in, f2_out, **ln(2), k0=6, emit_h=False)
        if not last:
            xc = ffn(xc, mods, ctx_row, f2_in, f2_out, **ln(2), k0=6, emit_h=False)

    return xl.reshape(batch, lat_chunk, LAT_CHUNKS, d).transpose(0, 2, 1, 3).reshape(batch, seq, d)
```

```python
import functools
import math

import jax
import jax.numpy as jnp
from jax import lax
from jax.experimental import pallas as pl
from jax.experimental.pallas import tpu as pltpu

F32 = jnp.float32
BF16 = jnp.bfloat16

SUBLANES = 8
LANES = 128
VMEM_BYTES_V7X = 64 * 1024 * 1024
VMEM_LIMIT = VMEM_BYTES_V7X - 8 * 1024 * 1024

GRID_W = 64
N_MOD = 9
RG_BLOCKS = 16
RG_CONV = 4
RG_PAD_LO = 2
SC_CONV = 3
SC_PAD_LO = 1
RG_C = 8.0
LN_EPS = 1e-5
N_PROJ = 7

SEG = GRID_W
ROW_TILE = SEG * SUBLANES
LAT_CHUNKS = SUBLANES
CTX_CHUNKS = 4
SCAN_BLOCK = 8
LN_ROWS = 128
FFN_ROW_TILE = 2 * ROW_TILE
MM_ROWS = 256


def _sigmoid(x):
    return 1.0 / (1.0 + jnp.exp(-x))


def _sigmoid_tanh(x):
    return 0.5 + 0.5 * jnp.tanh(0.5 * x)


def _gelu_tanh(x):
    return x * (0.5 * (1.0 + jnp.tanh(math.sqrt(2.0 / math.pi) * (x + 0.044715 * (x * x * x)))))


def _layer_norm(z, g, b):
    mu = jnp.mean(z, axis=-1, keepdims=True)
    zc = z - mu
    var = jnp.mean(zc * zc, axis=-1, keepdims=True)
    return zc * lax.rsqrt(var + LN_EPS) * g + b


def _params(*sem):
    return pltpu.CompilerParams(dimension_semantics=sem, vmem_limit_bytes=VMEM_LIMIT)


def _mod_kernel(c_ref, w_ref, b_ref, o_ref):
    @pl.when(pl.program_id(1) == 0)
    def _():
        o_ref[...] = jnp.broadcast_to(b_ref[...], o_ref.shape)

    c = c_ref[...]
    s = (c * _sigmoid(c)).astype(BF16)
    o_ref[...] += jnp.dot(s, w_ref[...].astype(BF16), preferred_element_type=F32)


def _mod_vectors(cond, w_mod, b_mod):
    depth, d, n = w_mod.shape
    tk = LANES
    return pl.pallas_call(
        _mod_kernel,
        grid=(depth, d // tk),
        in_specs=[
            pl.BlockSpec((SUBLANES, tk), lambda l, k: (0, k)),
            pl.BlockSpec((None, tk, n), lambda l, k: (l, k, 0)),
            pl.BlockSpec((None, 1, n), lambda l, k: (l, 0, 0)),
        ],
        out_specs=pl.BlockSpec((None, SUBLANES, n), lambda l, k: (l, 0, 0)),
        out_shape=jax.ShapeDtypeStruct((depth, SUBLANES, n), F32),
        compiler_params=_params("arbitrary", "arbitrary"),
        name="mod_vectors",
    )(cond, w_mod, b_mod.reshape(depth, 1, n))


def _ffn_kernel(x_hbm, mod_ref, wg_ref, wu_ref, wo_ref, lng_ref, lnb_ref, *rest, k0, alpha, emit_h):
    if emit_h:
        o_ref, h2_ref, x_buf, x_sem = rest
    else:
        o_ref, x_buf, x_sem = rest
    i = pl.program_id(0)
    f = pl.program_id(1)
    tm = x_buf.shape[1]
    slot = i % 2
    x_ref = x_buf.at[slot]
    m = mod_ref[...]
    scale1 = 1.0 + m[k0 + 1:k0 + 2]
    shift = m[k0:k0 + 1]

    def x_copy(tile, to_slot):
        rows = pl.ds(pl.multiple_of(tile * tm, tm), tm)
        return pltpu.make_async_copy(x_hbm.at[rows, :], x_buf.at[to_slot], x_sem.at[to_slot])

    @pl.when(f == 0)
    def _():
        @pl.when(i == 0)
        def _():
            x_copy(0, 0).start()

        x_copy(i, slot).wait()

        @pl.when(i + 1 < pl.num_programs(0))
        def _():
            x_copy(i + 1, 1 - slot).start()

    def hidden(rows):
        h = (x_ref[rows, :] * scale1 + shift).astype(BF16)
        g = jnp.dot(h, wg_ref[...], preferred_element_type=F32)
        u = jnp.dot(h, wu_ref[...], preferred_element_type=F32)
        a = (g * _sigmoid(g) * u).astype(BF16)
        return jnp.dot(a, wo_ref[...], preferred_element_type=F32)

    row_blocks = [slice(r0, r0 + MM_ROWS) for r0 in range(0, tm, MM_ROWS)]

    @pl.when(f == 0)
    def _():
        for rows in row_blocks:
            o_ref[rows, :] = hidden(rows)

    @pl.when(f > 0)
    def _():
        for rows in row_blocks:
            o_ref[rows, :] += hidden(rows)

    @pl.when(f == pl.num_programs(1) - 1)
    def _():
        half_gate = 0.5 * m[k0 + 2:k0 + 3]
        g_ln = lng_ref[...]
        b_ln = lnb_ref[...]
        if emit_h:
            scale2 = 1.0 + m[k0 + 4:k0 + 5]
            g2 = g_ln * scale2
            b2 = b_ln * scale2 + m[k0 + 3:k0 + 4]

        def finish_rows(sb, _):
            rows = pl.ds(pl.multiple_of(sb * LN_ROWS, LN_ROWS), LN_ROWS)
            z = alpha * x_ref[rows, :] + half_gate * o_ref[rows, :]
            mu = jnp.mean(z, axis=-1, keepdims=True)
            zc = z - mu
            var = jnp.mean(zc * zc, axis=-1, keepdims=True)
            zn = zc * lax.rsqrt(var + LN_EPS)
            o_ref[rows, :] = zn * g_ln + b_ln
            if emit_h:
                h2_ref[rows, :] = (zn * g2 + b2).astype(BF16)
            return 0

        lax.fori_loop(0, tm // LN_ROWS, finish_rows, 0)


def _ffn_sublayer(x, mods, mod_row, w_in, w_out, ln_g, ln_b, *, layer, ln_idx, k0, alpha, emit_h):
    t, d = x.shape
    f_dim = w_out.shape[1]
    tm, tf = min(FFN_ROW_TILE, t), 512
    nf = f_dim // tf
    out_shape = [jax.ShapeDtypeStruct((t, d), F32)]
    out_specs = [pl.BlockSpec((tm, d), lambda i, f: (i, 0))]
    if emit_h:
        out_shape.append(jax.ShapeDtypeStruct((t, d), BF16))
        out_specs.append(pl.BlockSpec((tm, d), lambda i, f: (i, 0)))
    res = pl.pallas_call(
        functools.partial(_ffn_kernel, k0=k0, alpha=alpha, emit_h=emit_h),
        grid=(t // tm, nf),
        in_specs=[
            pl.BlockSpec(memory_space=pl.ANY),
            pl.BlockSpec((None, None, N_MOD, d), lambda i, f: (layer, mod_row(i * tm), 0, 0)),
            pl.BlockSpec((None, d, tf), lambda i, f: (layer, 0, f)),
            pl.BlockSpec((None, d, tf), lambda i, f: (layer, 0, f + nf)),
            pl.BlockSpec((None, tf, d), lambda i, f: (layer, f, 0)),
            pl.BlockSpec((None, 1, d), lambda i, f: (ln_idx, 0, 0)),
            pl.BlockSpec((None, 1, d), lambda i, f: (ln_idx, 0, 0)),
        ],
        out_specs=out_specs,
        out_shape=out_shape,
        scratch_shapes=[pltpu.VMEM((2, tm, d), F32), pltpu.SemaphoreType.DMA((2,))],
        compiler_params=_params("arbitrary", "arbitrary"),
        name="ffn_sublayer",
    )(x, mods, w_in, w_in, w_out, ln_g, ln_b)
    return res if emit_h else res[0]


def _matmul_kernel(a_ref, b_ref, o_ref):
    o_ref[...] = jnp.dot(a_ref[...], b_ref[...], preferred_element_type=F32).astype(o_ref.dtype)


def _project(h, w, n_cols, *, layer):
    t, d = h.shape
    tm = min(t, 1024)
    tn = 2048
    return pl.pallas_call(
        _matmul_kernel,
        grid=(t // tm, n_cols // tn),
        in_specs=[
            pl.BlockSpec((tm, d), lambda i, j: (i, 0)),
            pl.BlockSpec((None, d, tn), lambda i, j: (layer, 0, j)),
        ],
        out_specs=pl.BlockSpec((tm, tn), lambda i, j: (i, j)),
        out_shape=jax.ShapeDtypeStruct((t, n_cols), BF16),
        compiler_params=_params("arbitrary", "arbitrary"),
        name="in_proj",
    )(h, w)


def _fill_halo(scr, x3, lo, hi, seq_chunks):
    n = x3.shape[0]
    scr[lo:lo + n] = x3
    tile = x3.shape[1:]
    if seq_chunks is None:
        if lo:
            scr[0:lo] = jnp.zeros((lo,) + tile, F32)
        if hi:
            scr[lo + n:lo + n + hi] = jnp.zeros((hi,) + tile, F32)
        return
    chunk = lax.broadcasted_iota(jnp.int32, tile, 0) % seq_chunks
    for t in range(lo):
        prev = pltpu.roll(x3[n - lo + t], 1, 0)
        scr[t] = jnp.where(chunk != 0, prev, 0.0)
    for t in range(hi):
        nxt = pltpu.roll(x3[t], SUBLANES - 1, 0)
        scr[lo + n + t] = jnp.where(chunk != seq_chunks - 1, nxt, 0.0)


def _conv_from_halo(scr, w, n):
    out = w[0:1] * scr[0:n]
    for k in range(1, w.shape[0]):
        out = out + w[k:k + 1] * scr[k:k + n]
    return out


def _rglru_kernel(p_ref, cw_ref, cb_ref, gw_ref, gb_ref, lam_ref, h0f_ref, h0b_ref, *rest,
                  chunk_len, seq_chunks, wrap, emit_state):
    if emit_state:
        h_ref, sf_ref, sb_ref, x_scr, a_f, v_f, a_b, v_b, h_f, p_f, h_b, p_b = rest
    else:
        h_ref, x_scr, a_f, v_f, a_b, v_b, h_f, p_f, h_b, p_b = rest
    n_seg = chunk_len // SEG
    cw = cw_ref[...]
    cb = cb_ref[...]
    lam = lam_ref[...]
    log_sig = jnp.minimum(lam, 0.0) - jnp.log1p(jnp.exp(-jnp.abs(lam)))
    e_fac = (0.5 * RG_C / math.log(2.0)) * log_sig
    gw_half = gw_ref[...] * 0.5
    gb_half = gb_ref[...] * 0.5
    a_scr = (a_f, a_b)
    v_scr = (v_f, v_b)

    def gates(si, _):
        r0 = pl.multiple_of(si * ROW_TILE, ROW_TILE)
        x3 = p_ref[pl.ds(r0, ROW_TILE), :].astype(F32).reshape(SEG, SUBLANES, LANES)
        _fill_halo(x_scr, x3, RG_PAD_LO, RG_CONV - 1 - RG_PAD_LO, seq_chunks if wrap else None)
        xc = (_conv_from_halo(x_scr, cw, SEG) + cb).reshape(ROW_TILE, LANES)
        t = jnp.tanh(jnp.dot(xc.astype(BF16), gw_half, preferred_element_type=F32) + gb_half)
        xh = 0.5 * xc
        j0 = pl.multiple_of(si * SEG, SEG)
        for d in range(2):
            t_r = t[:, (2 * d) * LANES:(2 * d + 1) * LANES]
            t_i = t[:, (2 * d + 1) * LANES:(2 * d + 2) * LANES]
            ef = e_fac[d:d + 1]
            a = jnp.exp2(ef * t_r + ef)
            y = 1.0 - a * a
            s = jnp.where(y > 0.0, y * lax.rsqrt(y), 0.0)
            v = s * (xh * t_i + xh)
            a_scr[d][pl.ds(j0, SEG)] = a.reshape(SEG, SUBLANES, LANES)
            v_scr[d][pl.ds(j0, SEG)] = v.reshape(SEG, SUBLANES, LANES)
        return 0

    lax.fori_loop(0, n_seg, gates, 0)

    def steps(blk, carry):
        hf, pf, hb, pb = carry
        jf = pl.ds(pl.multiple_of(blk * SCAN_BLOCK, SCAN_BLOCK), SCAN_BLOCK)
        jb = pl.ds(pl.multiple_of(chunk_len - SCAN_BLOCK - blk * SCAN_BLOCK, SCAN_BLOCK), SCAN_BLOCK)
        a, v = a_f[jf], v_f[jf]
        hs, ps = [], []
        for u in range(SCAN_BLOCK):
            hf = a[u] * hf + v[u]
            pf = a[u] * pf
            hs.append(hf)
            ps.append(pf)
        h_f[jf] = jnp.stack(hs)
        p_f[jf] = jnp.stack(ps)
        a, v = a_b[jb], v_b[jb]
        hs, ps = [], []
        for u in reversed(range(SCAN_BLOCK)):
            hb = a[u] * hb + v[u]
            pb = a[u] * pb
            hs.append(hb)
            ps.append(pb)
        h_b[jb] = jnp.stack(hs[::-1])
        p_b[jb] = jnp.stack(ps[::-1])
        return hf, pf, hb, pb

    zero = jnp.zeros((SUBLANES, LANES), F32)
    one = jnp.ones((SUBLANES, LANES), F32)
    hf, pf, hb, pb = lax.fori_loop(0, chunk_len // SCAN_BLOCK, steps, (zero, one, zero, one))

    sub = lax.broadcasted_iota(jnp.int32, (SUBLANES, LANES), 0)
    h0f = h0f_ref[...]
    h0b = h0b_ref[...]
    cf = jnp.zeros((SUBLANES, LANES), F32)
    row = None
    for s in range(SUBLANES):
        if s % seq_chunks == 0:
            row = h0f[s:s + 1]
        else:
            row = hf[s - 1:s] + pf[s - 1:s] * row
        cf = jnp.where(sub == s, row, cf)
    cbk = jnp.zeros((SUBLANES, LANES), F32)
    for s in reversed(range(SUBLANES)):
        if s % seq_chunks == seq_chunks - 1:
            row = h0b[s:s + 1]
        else:
            row = hb[s + 1:s + 2] + pb[s + 1:s + 2] * row
        cbk = jnp.where(sub == s, row, cbk)

    if emit_state:
        sf_ref[...] = hf + pf * cf
        sb_ref[...] = hb + pb * cbk

    def combine(si, _):
        j0 = pl.multiple_of(si * SEG, SEG)
        r0 = pl.multiple_of(si * ROW_TILE, ROW_TILE)
        js = pl.ds(j0, SEG)
        h = (h_f[js] + p_f[js] * cf) + (h_b[js] + p_b[js] * cbk)
        h_ref[pl.ds(r0, ROW_TILE), :] = h.reshape(ROW_TILE, LANES).astype(h_ref.dtype)
        return 0

    lax.fori_loop(0, n_seg, combine, 0)


def _rglru_branch(p, conv_w, conv_b, gate_w, gate_b, lam, h0f, h0b, *, layer, chunk_len, seq_chunks, wrap,
                  emit_state):
    t = p.shape[0]
    w = conv_w.shape[2]
    rows = chunk_len * SUBLANES
    groups = t // rows
    cb = LANES
    out_shape = [jax.ShapeDtypeStruct((t, w), BF16)]
    out_specs = [pl.BlockSpec((rows, cb), lambda g, c: (g, c))]
    if emit_state:
        for _ in range(2):
            out_shape.append(jax.ShapeDtypeStruct((groups, SUBLANES, w), F32))
            out_specs.append(pl.BlockSpec((None, SUBLANES, cb), lambda g, c: (g, 0, c)))
    return pl.pallas_call(
        functools.partial(_rglru_kernel, chunk_len=chunk_len, seq_chunks=seq_chunks, wrap=wrap,
                          emit_state=emit_state),
        grid=(groups, w // cb),
        in_specs=[
            pl.BlockSpec((rows, cb), lambda g, c: (g, c)),
            pl.BlockSpec((None, RG_CONV, cb), lambda g, c: (layer, 0, c)),
            pl.BlockSpec((None, 1, cb), lambda g, c: (layer, 0, c)),
            pl.BlockSpec((None, None, cb, 4 * cb), lambda g, c: (layer, c, 0, 0)),
            pl.BlockSpec((None, None, 1, 4 * cb), lambda g, c: (layer, c, 0, 0)),
            pl.BlockSpec((None, 2, cb), lambda g, c: (layer, 0, c)),
            pl.BlockSpec((None, SUBLANES, cb), lambda g, c: (g, 0, c)),
            pl.BlockSpec((None, SUBLANES, cb), lambda g, c: (g, 0, c)),
        ],
        out_specs=out_specs,
        out_shape=out_shape,
        scratch_shapes=[pltpu.VMEM((SEG + RG_CONV - 1, SUBLANES, cb), F32)]
        + [pltpu.VMEM((chunk_len, SUBLANES, cb), F32) for _ in range(8)],
        compiler_params=_params("arbitrary", "arbitrary"),
        name="rglru_branch",
    )(p, conv_w, conv_b, gate_w, gate_b, lam, h0f, h0b)


def _merge_kernel(hrg_ref, gate_ref, scb_ref, scc_ref, scx_ref, grg_ref, gsc_ref, scw_ref, bm_ref,
                  wrg_ref, wsc_ref, o_ref, a_scr, s_scr, q_scr, acc_rg, acc_sc, *, seq_chunks, wrap, nb):
    s = pl.program_id(0)
    last = pl.num_programs(0) - 1
    c_prev = (s + nb - 1) % nb
    tc = a_scr.shape[2]

    def produce(slot):
        a_scr[slot] = (hrg_ref[...].astype(F32) * _gelu_tanh(gate_ref[...].astype(F32))).astype(BF16)
        q = scc_ref[...].astype(F32) * scx_ref[...].astype(F32)
        _fill_halo(q_scr, q.reshape(SEG, SUBLANES, tc), SC_PAD_LO, SC_CONV - 1 - SC_PAD_LO,
                   seq_chunks if wrap else None)
        cq = _conv_from_halo(q_scr, scw_ref[...], SEG).reshape(ROW_TILE, tc)
        s_scr[slot] = (scb_ref[...].astype(F32) * cq).astype(BF16)

    def consume(slot):
        keep = c_prev != 0
        krows = pl.ds(pl.multiple_of(c_prev * tc, tc), tc)
        acc_rg[...] = (jnp.where(keep, acc_rg[...], 0.0)
                       + jnp.dot(a_scr[slot], wrg_ref[krows, :], preferred_element_type=F32))
        acc_sc[...] = (jnp.where(keep, acc_sc[...], 0.0)
                       + jnp.dot(s_scr[slot], wsc_ref[krows, :], preferred_element_type=F32))

    @pl.when(s == 0)
    def _():
        acc_rg[...] = jnp.zeros_like(acc_rg)
        acc_sc[...] = jnp.zeros_like(acc_sc)
        produce(0)

    @pl.when(jnp.logical_and(s > 0, s < last))
    def _():
        produce(s % 2)
        consume((s + 1) % 2)

    @pl.when(s == last)
    def _():
        consume((s + 1) % 2)

    @pl.when(jnp.logical_and(s > 0, c_prev == nb - 1))
    def _():
        bm = bm_ref[...]
        m = (_sigmoid_tanh(grg_ref[...].astype(F32) + bm[0:1]) * acc_rg[...]
             + _sigmoid_tanh(gsc_ref[...].astype(F32) + bm[1:2]) * acc_sc[...])
        o_ref[...] = m.astype(o_ref.dtype)


def _mixer_merge(p, h_rg, sc_conv_w, b_merge, w_rg_out, w_sc_out, *, layer, seq_chunks, wrap):
    t, d = h_rg.shape
    tm, tc = ROW_TILE, 512
    nb = d // tc
    n_steps = (t // tm) * nb
    built = lambda s: jnp.minimum(s, n_steps - 1)
    used = lambda s: jnp.maximum(s - 1, 0)
    chunk = lambda sec: pl.BlockSpec((tm, tc), lambda s: (built(s) // nb, sec * nb + built(s) % nb))
    resident = pl.BlockSpec((None, d, d), lambda s: (layer, 0, 0), pipeline_mode=pl.Buffered(1))
    return pl.pallas_call(
        functools.partial(_merge_kernel, seq_chunks=seq_chunks, wrap=wrap, nb=nb),
        grid=(n_steps + 1,),
        in_specs=[
            chunk(0), chunk(1), chunk(2), chunk(3), chunk(4),
            pl.BlockSpec((tm, d), lambda s: (used(s) // nb, 5)),
            pl.BlockSpec((tm, d), lambda s: (used(s) // nb, 6)),
            pl.BlockSpec((None, SC_CONV, tc), lambda s: (layer, 0, built(s) % nb)),
            pl.BlockSpec((None, 2, d), lambda s: (layer, 0, 0)),
            resident, resident,
        ],
        out_specs=pl.BlockSpec((tm, d), lambda s: (used(s) // nb, 0)),
        out_shape=jax.ShapeDtypeStruct((t, d), BF16),
        scratch_shapes=[pltpu.VMEM((2, tm, tc), BF16), pltpu.VMEM((2, tm, tc), BF16),
                        pltpu.VMEM((SEG + SC_CONV - 1, SUBLANES, tc), F32),
                        pltpu.VMEM((tm, d), F32), pltpu.VMEM((tm, d), F32)],
        compiler_params=_params("arbitrary"),
        name="mixer_merge",
    )(h_rg, p, p, p, p, p, p, sc_conv_w, b_merge, w_rg_out, w_sc_out)


def _oproj_kernel(m_ref, x_ref, mod_ref, wo_ref, lng_ref, lnb_ref, o_ref, *, alpha):
    gate = mod_ref[...][5:6]
    wo = wo_ref[...]
    g_ln = lng_ref[...]
    b_ln = lnb_ref[...]
    for r0 in range(0, m_ref.shape[0], LN_ROWS):
        rows = slice(r0, r0 + LN_ROWS)
        y = jnp.dot(m_ref[rows, :], wo, preferred_element_type=F32)
        z = alpha * x_ref[rows, :] + gate * y
        o_ref[rows, :] = _layer_norm(z, g_ln, b_ln)


def _mixer_oproj(m, x, mods, mod_row, w_o, ln_g, ln_b, *, layer, ln_idx, alpha):
    t, d = x.shape
    tm = ROW_TILE
    return pl.pallas_call(
        functools.partial(_oproj_kernel, alpha=alpha),
        grid=(t // tm,),
        in_specs=[
            pl.BlockSpec((tm, d), lambda i: (i, 0)),
            pl.BlockSpec((tm, d), lambda i: (i, 0)),
            pl.BlockSpec((None, None, N_MOD, d), lambda i: (layer, mod_row(i * tm), 0, 0)),
            pl.BlockSpec((None, d, d), lambda i: (layer, 0, 0)),
            pl.BlockSpec((None, 1, d), lambda i: (ln_idx, 0, 0)),
            pl.BlockSpec((None, 1, d), lambda i: (ln_idx, 0, 0)),
        ],
        out_specs=pl.BlockSpec((tm, d), lambda i: (i, 0)),
        out_shape=jax.ShapeDtypeStruct((t, d), F32),
        compiler_params=_params("arbitrary"),
        name="mixer_oproj",
    )(m, x, mods, w_o, ln_g, ln_b)


def kernel(x, c, ctx, c_ctx, w_mod, b_mod, ln_g, ln_b, ffn1_w_in, ffn1_w_out, ffn2_w_in, ffn2_w_out, w_in, rg_conv_w, rg_conv_b, rg_gate_w, rg_gate_b, rg_lam, sc_conv_w, w_rg_out, w_sc_out, b_merge, w_o):
    batch, seq, d = x.shape
    ctx_len = ctx.shape[1]
    depth = w_mod.shape[0]
    w_rg = rg_conv_w.shape[2]
    n_ln = ln_g.shape[1]
    alpha = (2 * depth) ** 0.25
    lat_chunk = seq // LAT_CHUNKS
    ctx_chunk = ctx_len // CTX_CHUNKS
    ctx_group = SUBLANES // CTX_CHUNKS
    assert lat_chunk % GRID_W == 0 and ctx_chunk == SEG and batch % ctx_group == 0
    assert w_rg == RG_BLOCKS * LANES and d % LANES == 0

    xl = x.reshape(batch, LAT_CHUNKS, lat_chunk, d).transpose(0, 2, 1, 3).reshape(batch * seq, d)
    xc = ctx.reshape(batch // ctx_group, ctx_group, CTX_CHUNKS, ctx_chunk, d).transpose(0, 3, 1, 2, 4)
    xc = xc.reshape(batch * ctx_len, d)

    cond = jnp.zeros((SUBLANES, d), F32).at[:batch].set(c).at[batch].set(c_ctx)
    mods = _mod_vectors(cond, w_mod, b_mod).reshape(depth, SUBLANES, N_MOD, d)
    lat_row = lambda r: r // seq
    ctx_row = lambda r: batch

    bf = lambda a: a.astype(BF16)
    f1_in, f1_out, f2_in, f2_out = bf(ffn1_w_in), bf(ffn1_w_out), bf(ffn2_w_in), bf(ffn2_w_out)
    w_in_b, w_rg_out_b, w_sc_out_b, w_o_b = bf(w_in), bf(w_rg_out), bf(w_sc_out), bf(w_o)
    ln_g = ln_g.reshape(depth * n_ln, 1, d)
    ln_b = ln_b.reshape(depth * n_ln, 1, d)
    gw = bf(rg_gate_w.transpose(0, 3, 4, 1, 2, 5).reshape(depth, RG_BLOCKS, LANES, 4 * LANES))
    gb = rg_gate_b.reshape(depth, 4, RG_BLOCKS, LANES).transpose(0, 2, 1, 3).reshape(depth, RG_BLOCKS, 1, 4 * LANES)
    rg_args = (rg_conv_w, rg_conv_b.reshape(depth, 1, w_rg), gw, gb, rg_lam)
    zeros_state = jnp.zeros((batch // ctx_group, SUBLANES, w_rg), F32)

    for l in range(depth):
        last = l == depth - 1
        ffn = functools.partial(_ffn_sublayer, layer=l, alpha=alpha)
        ln = lambda k: dict(ln_g=ln_g, ln_b=ln_b, ln_idx=l * n_ln + k)

        xl, hl = ffn(xl, mods, lat_row, f1_in, f1_out, **ln(0), k0=0, emit_h=True)
        xc, hc = ffn(xc, mods, ctx_row, f1_in, f1_out, **ln(0), k0=0, emit_h=True)

        pc = _project(hc, w_in_b, w_rg if last else N_PROJ * d, layer=l)
        hrg_c, s_f, s_b = _rglru_branch(pc, *rg_args, zeros_state, zeros_state, layer=l, chunk_len=ctx_chunk,
                                        seq_chunks=CTX_CHUNKS, wrap=True, emit_state=True)
        if not last:
            mc = _mixer_merge(pc, hrg_c, sc_conv_w, b_merge, w_rg_out_b, w_sc_out_b, layer=l,
                              seq_chunks=CTX_CHUNKS, wrap=True)
            xc = _mixer_oproj(mc, xc, mods, ctx_row, w_o_b, **ln(1), layer=l, alpha=alpha)
        s_f = s_f.reshape(batch, CTX_CHUNKS, w_rg)[:, CTX_CHUNKS - 1]
        s_b = s_b.reshape(batch, CTX_CHUNKS, w_rg)[:, 0]
        h0f = jnp.broadcast_to(s_f[:, None, :], (batch, SUBLANES, w_rg))
        h0b = jnp.broadcast_to(s_b[:, None, :], (batch, SUBLANES, w_rg))

        pl_ = _project(hl, w_in_b, N_PROJ * d, layer=l)
        (hrg_l,) = _rglru_branch(pl_, *rg_args, h0f, h0b, layer=l, chunk_len=lat_chunk, seq_chunks=LAT_CHUNKS,
                                 wrap=False, emit_state=False)
        ml = _mixer_merge(pl_, hrg_l, sc_conv_w, b_merge, w_rg_out_b, w_sc_out_b, layer=l,
                          seq_chunks=LAT_CHUNKS, wrap=False)
        xl = _mixer_oproj(ml, xl, mods, lat_row, w_o_b, **ln(1), layer=l, alpha=alpha)

        xl = ffn(xl, mods, lat_row, f2_in, f2_out, **ln(2), k0=6, emit_h=False)
        if not last:
            xc = ffn(xc, mods, ctx_row, f2_in, f2_out, **ln(2), k0=6, emit_h=False)

    return xl.reshape(batch, lat_chunk, LAT_CHUNKS, d).transpose(0, 2, 1, 3).reshape(batch, seq, d)
```

```python
import functools
import math

import jax
import jax.numpy as jnp
from jax import lax
from jax.experimental import pallas as pl
from jax.experimental.pallas import tpu as pltpu

F32 = jnp.float32
BF16 = jnp.bfloat16

SUBLANES = 8
LANES = 128
VMEM_BYTES_V7X = 64 * 1024 * 1024
VMEM_LIMIT = VMEM_BYTES_V7X - 8 * 1024 * 1024

GRID_W = 64
N_MOD = 9
RG_BLOCKS = 16
RG_CONV = 4
RG_PAD_LO = 2
SC_CONV = 3
SC_PAD_LO = 1
RG_C = 8.0
LN_EPS = 1e-5
N_PROJ = 7

SEG = GRID_W
ROW_TILE = SEG * SUBLANES
LAT_CHUNKS = SUBLANES
CTX_CHUNKS = 4
SCAN_BLOCK = 8
LN_ROWS = 128
FFN_ROW_TILE = 2 * ROW_TILE
MM_ROWS = 256
PROJ_ROW_TILE = 4 * ROW_TILE
RG_LANES = 2 * LANES


def _sigmoid(x):
    return 1.0 / (1.0 + jnp.exp(-x))


def _sigmoid_tanh(x):
    return 0.5 + 0.5 * jnp.tanh(0.5 * x)


def _gelu_tanh(x):
    return x * (0.5 * (1.0 + jnp.tanh(math.sqrt(2.0 / math.pi) * (x + 0.044715 * (x * x * x)))))


def _layer_norm(z, g, b):
    mu = jnp.mean(z, axis=-1, keepdims=True)
    zc = z - mu
    var = jnp.mean(zc * zc, axis=-1, keepdims=True)
    return zc * lax.rsqrt(var + LN_EPS) * g + b


def _params(*sem):
    return pltpu.CompilerParams(dimension_semantics=sem, vmem_limit_bytes=VMEM_LIMIT)


def _mod_kernel(c_ref, w_ref, b_ref, o_ref):
    @pl.when(pl.program_id(1) == 0)
    def _():
        o_ref[...] = jnp.broadcast_to(b_ref[...], o_ref.shape)

    c = c_ref[...]
    s = (c * _sigmoid(c)).astype(BF16)
    o_ref[...] += jnp.dot(s, w_ref[...].astype(BF16), preferred_element_type=F32)


def _mod_vectors(cond, w_mod, b_mod):
    depth, d, n = w_mod.shape
    tk = LANES
    return pl.pallas_call(
        _mod_kernel,
        grid=(depth, d // tk),
        in_specs=[
            pl.BlockSpec((SUBLANES, tk), lambda l, k: (0, k)),
            pl.BlockSpec((None, tk, n), lambda l, k: (l, k, 0)),
            pl.BlockSpec((None, 1, n), lambda l, k: (l, 0, 0)),
        ],
        out_specs=pl.BlockSpec((None, SUBLANES, n), lambda l, k: (l, 0, 0)),
        out_shape=jax.ShapeDtypeStruct((depth, SUBLANES, n), F32),
        compiler_params=_params("arbitrary", "arbitrary"),
        name="mod_vectors",
    )(cond, w_mod, b_mod.reshape(depth, 1, n))


def _ffn_kernel(x_ref, mod_ref, wg_ref, wu_ref, wo_ref, lng_ref, lnb_ref, *rest, k0, alpha, emit_h):
    if emit_h:
        o_ref, h2_ref = rest
    else:
        (o_ref,) = rest
    f = pl.program_id(1)
    tm = x_ref.shape[0]
    m = mod_ref[...]
    scale1 = 1.0 + m[k0 + 1:k0 + 2]
    shift = m[k0:k0 + 1]

    def hidden(rows):
        h = (x_ref[rows, :] * scale1 + shift).astype(BF16)
        g = jnp.dot(h, wg_ref[...], preferred_element_type=F32)
        u = jnp.dot(h, wu_ref[...], preferred_element_type=F32)
        a = (g * _sigmoid(g) * u).astype(BF16)
        return jnp.dot(a, wo_ref[...], preferred_element_type=F32)

    row_blocks = [slice(r0, r0 + MM_ROWS) for r0 in range(0, tm, MM_ROWS)]

    @pl.when(f == 0)
    def _():
        for rows in row_blocks:
            o_ref[rows, :] = hidden(rows)

    @pl.when(f > 0)
    def _():
        for rows in row_blocks:
            o_ref[rows, :] += hidden(rows)

    @pl.when(f == pl.num_programs(1) - 1)
    def _():
        half_gate = 0.5 * m[k0 + 2:k0 + 3]
        g_ln = lng_ref[...]
        b_ln = lnb_ref[...]
        if emit_h:
            scale2 = 1.0 + m[k0 + 4:k0 + 5]
            g2 = g_ln * scale2
            b2 = b_ln * scale2 + m[k0 + 3:k0 + 4]

        def finish_rows(sb, _):
            rows = pl.ds(pl.multiple_of(sb * LN_ROWS, LN_ROWS), LN_ROWS)
            z = alpha * x_ref[rows, :] + half_gate * o_ref[rows, :]
            mu = jnp.mean(z, axis=-1, keepdims=True)
            zc = z - mu
            var = jnp.mean(zc * zc, axis=-1, keepdims=True)
            zn = zc * lax.rsqrt(var + LN_EPS)
            o_ref[rows, :] = zn * g_ln + b_ln
            if emit_h:
                h2_ref[rows, :] = (zn * g2 + b2).astype(BF16)
            return 0

        lax.fori_loop(0, tm // LN_ROWS, finish_rows, 0)


def _ffn_sublayer(x, mods, mod_row, w_in, w_out, ln_g, ln_b, *, layer, ln_idx, k0, alpha, emit_h):
    t, d = x.shape
    f_dim = w_out.shape[1]
    tm, tf = min(FFN_ROW_TILE, t), 512
    nf = f_dim // tf
    out_shape = [jax.ShapeDtypeStruct((t, d), F32)]
    out_specs = [pl.BlockSpec((tm, d), lambda i, f: (i, 0))]
    if emit_h:
        out_shape.append(jax.ShapeDtypeStruct((t, d), BF16))
        out_specs.append(pl.BlockSpec((tm, d), lambda i, f: (i, 0)))
    res = pl.pallas_call(
        functools.partial(_ffn_kernel, k0=k0, alpha=alpha, emit_h=emit_h),
        grid=(t // tm, nf),
        in_specs=[
            pl.BlockSpec((tm, d), lambda i, f: (i, 0)),
            pl.BlockSpec((None, None, N_MOD, d), lambda i, f: (layer, mod_row(i * tm), 0, 0)),
            pl.BlockSpec((None, d, tf), lambda i, f: (layer, 0, f)),
            pl.BlockSpec((None, d, tf), lambda i, f: (layer, 0, f + nf)),
            pl.BlockSpec((None, tf, d), lambda i, f: (layer, f, 0)),
            pl.BlockSpec((None, 1, d), lambda i, f: (ln_idx, 0, 0)),
            pl.BlockSpec((None, 1, d), lambda i, f: (ln_idx, 0, 0)),
        ],
        out_specs=out_specs,
        out_shape=out_shape,
        compiler_params=_params("arbitrary", "arbitrary"),
        name="ffn_sublayer",
    )(x, mods, w_in, w_in, w_out, ln_g, ln_b)
    return res if emit_h else res[0]


def _matmul_kernel(a_ref, b_ref, o_ref):
    o_ref[...] = jnp.dot(a_ref[...], b_ref[...].astype(BF16), preferred_element_type=F32).astype(o_ref.dtype)


def _project(h, w, n_cols, *, layer):
    t, d = h.shape
    tm = min(t, PROJ_ROW_TILE)
    tn = 1024
    return pl.pallas_call(
        _matmul_kernel,
        grid=(t // tm, n_cols // tn),
        in_specs=[
            pl.BlockSpec((tm, d), lambda i, j: (i, 0)),
            pl.BlockSpec((None, d, tn), lambda i, j: (layer, 0, j)),
        ],
        out_specs=pl.BlockSpec((tm, tn), lambda i, j: (i, j)),
        out_shape=jax.ShapeDtypeStruct((t, n_cols), BF16),
        compiler_params=_params("arbitrary", "arbitrary"),
        name="in_proj",
    )(h, w)


def _fill_halo(scr, x3, lo, hi, seq_chunks):
    n = x3.shape[0]
    scr[lo:lo + n] = x3
    tile = x3.shape[1:]
    if seq_chunks is None:
        if lo:
            scr[0:lo] = jnp.zeros((lo,) + tile, F32)
        if hi:
            scr[lo + n:lo + n + hi] = jnp.zeros((hi,) + tile, F32)
        return
    chunk = lax.broadcasted_iota(jnp.int32, tile, 0) % seq_chunks
    for t in range(lo):
        prev = pltpu.roll(x3[n - lo + t], 1, 0)
        scr[t] = jnp.where(chunk != 0, prev, 0.0)
    for t in range(hi):
        nxt = pltpu.roll(x3[t], SUBLANES - 1, 0)
        scr[lo + n + t] = jnp.where(chunk != seq_chunks - 1, nxt, 0.0)


def _conv_from_halo(scr, w, n):
    out = w[0:1] * scr[0:n]
    for k in range(1, w.shape[0]):
        out = out + w[k:k + 1] * scr[k:k + n]
    return out


def _rglru_kernel(p_ref, cw_ref, cb_ref, gw_ref, gb_ref, lam_ref, h0f_ref, h0b_ref, *rest,
                  chunk_len, seq_chunks, wrap, emit_state):
    if emit_state:
        h_ref, sf_ref, sb_ref, x_scr, a_f, v_f, a_b, v_b, h_f, p_f, h_b, p_b = rest
    else:
        h_ref, x_scr, a_f, v_f, a_b, v_b, h_f, p_f, h_b, p_b = rest
    n_seg = chunk_len // SEG
    width = p_ref.shape[1]
    cw = cw_ref[...]
    cb = cb_ref[...]
    lam = lam_ref[...]
    log_sig = jnp.minimum(lam, 0.0) - jnp.log1p(jnp.exp(-jnp.abs(lam)))
    e_fac = (0.5 * RG_C / math.log(2.0)) * log_sig
    a_scr = (a_f, a_b)
    v_scr = (v_f, v_b)

    def gates(si, _):
        r0 = pl.multiple_of(si * ROW_TILE, ROW_TILE)
        j0 = pl.multiple_of(si * SEG, SEG)
        for blk in range(width // LANES):
            cs = slice(blk * LANES, (blk + 1) * LANES)
            gw_half = gw_ref[blk] * 0.5
            gb_half = gb_ref[blk] * 0.5
            x3 = p_ref[pl.ds(r0, ROW_TILE), cs].astype(F32).reshape(SEG, SUBLANES, LANES)
            _fill_halo(x_scr, x3, RG_PAD_LO, RG_CONV - 1 - RG_PAD_LO, seq_chunks if wrap else None)
            xc = (_conv_from_halo(x_scr, cw[:, cs], SEG) + cb[:, cs]).reshape(ROW_TILE, LANES)
            t = jnp.tanh(jnp.dot(xc.astype(BF16), gw_half, preferred_element_type=F32) + gb_half)
            xh = 0.5 * xc
            for d in range(2):
                t_r = t[:, (2 * d) * LANES:(2 * d + 1) * LANES]
                t_i = t[:, (2 * d + 1) * LANES:(2 * d + 2) * LANES]
                ef = e_fac[d:d + 1, cs]
                a = jnp.exp2(ef * t_r + ef)
                y = 1.0 - a * a
                s = jnp.where(y > 0.0, y * lax.rsqrt(y), 0.0)
                v = s * (xh * t_i + xh)
                a_scr[d][pl.ds(j0, SEG), :, cs] = a.reshape(SEG, SUBLANES, LANES)
                v_scr[d][pl.ds(j0, SEG), :, cs] = v.reshape(SEG, SUBLANES, LANES)
        return 0

    lax.fori_loop(0, n_seg, gates, 0)

    def steps(blk, carry):
        hf, pf, hb, pb = carry
        jf = pl.ds(pl.multiple_of(blk * SCAN_BLOCK, SCAN_BLOCK), SCAN_BLOCK)
        jb = pl.ds(pl.multiple_of(chunk_len - SCAN_BLOCK - blk * SCAN_BLOCK, SCAN_BLOCK), SCAN_BLOCK)
        a, v = a_f[jf], v_f[jf]
        hs, ps = [], []
        for u in range(SCAN_BLOCK):
            hf = a[u] * hf + v[u]
            pf = a[u] * pf
            hs.append(hf)
            ps.append(pf)
        h_f[jf] = jnp.stack(hs)
        p_f[jf] = jnp.stack(ps)
        a, v = a_b[jb], v_b[jb]
        hs, ps = [], []
        for u in reversed(range(SCAN_BLOCK)):
            hb = a[u] * hb + v[u]
            pb = a[u] * pb
            hs.append(hb)
            ps.append(pb)
        h_b[jb] = jnp.stack(hs[::-1])
        p_b[jb] = jnp.stack(ps[::-1])
        return hf, pf, hb, pb

    zero = jnp.zeros((SUBLANES, width), F32)
    one = jnp.ones((SUBLANES, width), F32)
    hf, pf, hb, pb = lax.fori_loop(0, chunk_len // SCAN_BLOCK, steps, (zero, one, zero, one))

    sub = lax.broadcasted_iota(jnp.int32, (SUBLANES, width), 0)
    h0f = h0f_ref[...]
    h0b = h0b_ref[...]
    cf = zero
    row = None
    for s in range(SUBLANES):
        if s % seq_chunks == 0:
            row = h0f[s:s + 1]
        else:
            row = hf[s - 1:s] + pf[s - 1:s] * row
        cf = jnp.where(sub == s, row, cf)
    cbk = zero
    for s in reversed(range(SUBLANES)):
        if s % seq_chunks == seq_chunks - 1:
            row = h0b[s:s + 1]
        else:
            row = hb[s + 1:s + 2] + pb[s + 1:s + 2] * row
        cbk = jnp.where(sub == s, row, cbk)

    if emit_state:
        sf_ref[...] = hf + pf * cf
        sb_ref[...] = hb + pb * cbk

    def combine(si, _):
        j0 = pl.multiple_of(si * SEG, SEG)
        r0 = pl.multiple_of(si * ROW_TILE, ROW_TILE)
        js = pl.ds(j0, SEG)
        h = (h_f[js] + p_f[js] * cf) + (h_b[js] + p_b[js] * cbk)
        h_ref[pl.ds(r0, ROW_TILE), :] = h.reshape(ROW_TILE, width).astype(h_ref.dtype)
        return 0

    lax.fori_loop(0, n_seg, combine, 0)


def _rglru_branch(p, conv_w, conv_b, gate_w, gate_b, lam, h0f, h0b, *, layer, chunk_len, seq_chunks, wrap,
                  emit_state):
    t = p.shape[0]
    w = conv_w.shape[2]
    rows = chunk_len * SUBLANES
    groups = t // rows
    cb = RG_LANES
    n_blk = cb // LANES
    out_shape = [jax.ShapeDtypeStruct((t, w), BF16)]
    out_specs = [pl.BlockSpec((rows, cb), lambda g, c: (g, c))]
    if emit_state:
        for _ in range(2):
            out_shape.append(jax.ShapeDtypeStruct((groups, SUBLANES, w), F32))
            out_specs.append(pl.BlockSpec((None, SUBLANES, cb), lambda g, c: (g, 0, c)))
    return pl.pallas_call(
        functools.partial(_rglru_kernel, chunk_len=chunk_len, seq_chunks=seq_chunks, wrap=wrap,
                          emit_state=emit_state),
        grid=(groups, w // cb),
        in_specs=[
            pl.BlockSpec((rows, cb), lambda g, c: (g, c)),
            pl.BlockSpec((None, RG_CONV, cb), lambda g, c: (layer, 0, c)),
            pl.BlockSpec((None, 1, cb), lambda g, c: (layer, 0, c)),
            pl.BlockSpec((None, n_blk, LANES, 4 * LANES), lambda g, c: (layer, c, 0, 0)),
            pl.BlockSpec((None, n_blk, 1, 4 * LANES), lambda g, c: (layer, c, 0, 0)),
            pl.BlockSpec((None, 2, cb), lambda g, c: (layer, 0, c)),
            pl.BlockSpec((None, SUBLANES, cb), lambda g, c: (g, 0, c)),
            pl.BlockSpec((None, SUBLANES, cb), lambda g, c: (g, 0, c)),
        ],
        out_specs=out_specs,
        out_shape=out_shape,
        scratch_shapes=[pltpu.VMEM((SEG + RG_CONV - 1, SUBLANES, LANES), F32)]
        + [pltpu.VMEM((chunk_len, SUBLANES, cb), F32) for _ in range(8)],
        compiler_params=_params("arbitrary", "arbitrary"),
        name="rglru_branch",
    )(p, conv_w, conv_b, gate_w, gate_b, lam, h0f, h0b)


def _merge_kernel(hrg_ref, gate_ref, scb_ref, scc_ref, scx_ref, grg_ref, gsc_ref, scw_ref, bm_ref,
                  wrg_ref, wsc_ref, o_ref, a_scr, s_scr, q_scr, acc_rg, acc_sc, *, seq_chunks, wrap, nb):
    s = pl.program_id(0)
    last = pl.num_programs(0) - 1
    c_prev = (s + nb - 1) % nb
    tc = a_scr.shape[2]

    def produce(slot):
        a_scr[slot] = (hrg_ref[...].astype(F32) * _gelu_tanh(gate_ref[...].astype(F32))).astype(BF16)
        q = scc_ref[...].astype(F32) * scx_ref[...].astype(F32)
        _fill_halo(q_scr, q.reshape(SEG, SUBLANES, tc), SC_PAD_LO, SC_CONV - 1 - SC_PAD_LO,
                   seq_chunks if wrap else None)
        cq = _conv_from_halo(q_scr, scw_ref[...], SEG).reshape(ROW_TILE, tc)
        s_scr[slot] = (scb_ref[...].astype(F32) * cq).astype(BF16)

    def consume(slot):
        keep = c_prev != 0
        krows = pl.ds(pl.multiple_of(c_prev * tc, tc), tc)
        acc_rg[...] = (jnp.where(keep, acc_rg[...], 0.0)
                       + jnp.dot(a_scr[slot], wrg_ref[krows, :], preferred_element_type=F32))
        acc_sc[...] = (jnp.where(keep, acc_sc[...], 0.0)
                       + jnp.dot(s_scr[slot], wsc_ref[krows, :], preferred_element_type=F32))

    @pl.when(s == 0)
    def _():
        acc_rg[...] = jnp.zeros_like(acc_rg)
        acc_sc[...] = jnp.zeros_like(acc_sc)
        produce(0)

    @pl.when(jnp.logical_and(s > 0, s < last))
    def _():
        produce(s % 2)
        consume((s + 1) % 2)

    @pl.when(s == last)
    def _():
        consume((s + 1) % 2)

    @pl.when(jnp.logical_and(s > 0, c_prev == nb - 1))
    def _():
        bm = bm_ref[...]
        m = (_sigmoid_tanh(grg_ref[...].astype(F32) + bm[0:1]) * acc_rg[...]
             + _sigmoid_tanh(gsc_ref[...].astype(F32) + bm[1:2]) * acc_sc[...])
        o_ref[...] = m.astype(o_ref.dtype)


def _mixer_merge(p, h_rg, sc_conv_w, b_merge, w_rg_out, w_sc_out, *, layer, seq_chunks, wrap):
    t, d = h_rg.shape
    tm, tc = ROW_TILE, 512
    nb = d // tc
    n_steps = (t // tm) * nb
    built = lambda s: jnp.minimum(s, n_steps - 1)
    used = lambda s: jnp.maximum(s - 1, 0)
    chunk = lambda sec: pl.BlockSpec((tm, tc), lambda s: (built(s) // nb, sec * nb + built(s) % nb))
    resident = pl.BlockSpec((None, d, d), lambda s: (layer, 0, 0), pipeline_mode=pl.Buffered(1))
    return pl.pallas_call(
        functools.partial(_merge_kernel, seq_chunks=seq_chunks, wrap=wrap, nb=nb),
        grid=(n_steps + 1,),
        in_specs=[
            chunk(0), chunk(1), chunk(2), chunk(3), chunk(4),
            pl.BlockSpec((tm, d), lambda s: (used(s) // nb, 5)),
            pl.BlockSpec((tm, d), lambda s: (used(s) // nb, 6)),
            pl.BlockSpec((None, SC_CONV, tc), lambda s: (layer, 0, built(s) % nb)),
            pl.BlockSpec((None, 2, d), lambda s: (layer, 0, 0)),
            resident, resident,
        ],
        out_specs=pl.BlockSpec((tm, d), lambda s: (used(s) // nb, 0)),
        out_shape=jax.ShapeDtypeStruct((t, d), BF16),
        scratch_shapes=[pltpu.VMEM((2, tm, tc), BF16), pltpu.VMEM((2, tm, tc), BF16),
                        pltpu.VMEM((SEG + SC_CONV - 1, SUBLANES, tc), F32),
                        pltpu.VMEM((tm, d), F32), pltpu.VMEM((tm, d), F32)],
        compiler_params=_params("arbitrary"),
        name="mixer_merge",
    )(h_rg, p, p, p, p, p, p, sc_conv_w, b_merge, w_rg_out, w_sc_out)


def _oproj_kernel(m_ref, x_ref, mod_ref, wo_ref, lng_ref, lnb_ref, o_ref, *, alpha):
    gate = mod_ref[...][5:6]
    wo = wo_ref[...]
    g_ln = lng_ref[...]
    b_ln = lnb_ref[...]
    for r0 in range(0, m_ref.shape[0], LN_ROWS):
        rows = slice(r0, r0 + LN_ROWS)
        y = jnp.dot(m_ref[rows, :], wo, preferred_element_type=F32)
        z = alpha * x_ref[rows, :] + gate * y
        o_ref[rows, :] = _layer_norm(z, g_ln, b_ln)


def _mixer_oproj(m, x, mods, mod_row, w_o, ln_g, ln_b, *, layer, ln_idx, alpha):
    t, d = x.shape
    tm = ROW_TILE
    return pl.pallas_call(
        functools.partial(_oproj_kernel, alpha=alpha),
        grid=(t // tm,),
        in_specs=[
            pl.BlockSpec((tm, d), lambda i: (i, 0)),
            pl.BlockSpec((tm, d), lambda i: (i, 0)),
            pl.BlockSpec((None, None, N_MOD, d), lambda i: (layer, mod_row(i * tm), 0, 0)),
            pl.BlockSpec((None, d, d), lambda i: (layer, 0, 0)),
            pl.BlockSpec((None, 1, d), lambda i: (ln_idx, 0, 0)),
            pl.BlockSpec((None, 1, d), lambda i: (ln_idx, 0, 0)),
        ],
        out_specs=pl.BlockSpec((tm, d), lambda i: (i, 0)),
        out_shape=jax.ShapeDtypeStruct((t, d), F32),
        compiler_params=_params("arbitrary"),
        name="mixer_oproj",
    )(m, x, mods, w_o, ln_g, ln_b)


def kernel(x, c, ctx, c_ctx, w_mod, b_mod, ln_g, ln_b, ffn1_w_in, ffn1_w_out, ffn2_w_in, ffn2_w_out, w_in, rg_conv_w, rg_conv_b, rg_gate_w, rg_gate_b, rg_lam, sc_conv_w, w_rg_out, w_sc_out, b_merge, w_o):
    batch, seq, d = x.shape
    ctx_len = ctx.shape[1]
    depth = w_mod.shape[0]
    w_rg = rg_conv_w.shape[2]
    n_ln = ln_g.shape[1]
    alpha = (2 * depth) ** 0.25
    lat_chunk = seq // LAT_CHUNKS
    ctx_chunk = ctx_len // CTX_CHUNKS
    ctx_group = SUBLANES // CTX_CHUNKS
    assert lat_chunk % GRID_W == 0 and ctx_chunk == SEG and batch % ctx_group == 0
    assert w_rg == RG_BLOCKS * LANES and d % LANES == 0

    xl = x.reshape(batch, LAT_CHUNKS, lat_chunk, d).transpose(0, 2, 1, 3).reshape(batch * seq, d)
    xc = ctx.reshape(batch // ctx_group, ctx_group, CTX_CHUNKS, ctx_chunk, d).transpose(0, 3, 1, 2, 4)
    xc = xc.reshape(batch * ctx_len, d)

    cond = jnp.zeros((SUBLANES, d), F32).at[:batch].set(c).at[batch].set(c_ctx)
    mods = _mod_vectors(cond, w_mod, b_mod).reshape(depth, SUBLANES, N_MOD, d)
    lat_row = lambda r: r // seq
    ctx_row = lambda r: batch

    bf = lambda a: a.astype(BF16)
    f1_in, f1_out, f2_in, f2_out = bf(ffn1_w_in), bf(ffn1_w_out), bf(ffn2_w_in), bf(ffn2_w_out)
    w_rg_out_b, w_sc_out_b, w_o_b = bf(w_rg_out), bf(w_sc_out), bf(w_o)
    ln_g = ln_g.reshape(depth * n_ln, 1, d)
    ln_b = ln_b.reshape(depth * n_ln, 1, d)
    gw = bf(rg_gate_w.transpose(0, 3, 4, 1, 2, 5).reshape(depth, RG_BLOCKS, LANES, 4 * LANES))
    gb = rg_gate_b.reshape(depth, 4, RG_BLOCKS, LANES).transpose(0, 2, 1, 3).reshape(depth, RG_BLOCKS, 1, 4 * LANES)
    rg_args = (rg_conv_w, rg_conv_b.reshape(depth, 1, w_rg), gw, gb, rg_lam)
    zeros_state = jnp.zeros((batch // ctx_group, SUBLANES, w_rg), F32)

    for l in range(depth):
        last = l == depth - 1
        ffn = functools.partial(_ffn_sublayer, layer=l, alpha=alpha)
        ln = lambda k: dict(ln_g=ln_g, ln_b=ln_b, ln_idx=l * n_ln + k)

        xl, hl = ffn(xl, mods, lat_row, f1_in, f1_out, **ln(0), k0=0, emit_h=True)
        xc, hc = ffn(xc, mods, ctx_row, f1_in, f1_out, **ln(0), k0=0, emit_h=True)

        pc = _project(hc, w_in, w_rg if last else N_PROJ * d, layer=l)
        hrg_c, s_f, s_b = _rglru_branch(pc, *rg_args, zeros_state, zeros_state, layer=l, chunk_len=ctx_chunk,
                                        seq_chunks=CTX_CHUNKS, wrap=True, emit_state=True)
        if not last:
            mc = _mixer_merge(pc, hrg_c, sc_conv_w, b_merge, w_rg_out_b, w_sc_out_b, layer=l,
                              seq_chunks=CTX_CHUNKS, wrap=True)
            xc = _mixer_oproj(mc, xc, mods, ctx_row, w_o_b, **ln(1), layer=l, alpha=alpha)
        s_f = s_f.reshape(batch, CTX_CHUNKS, w_rg)[:, CTX_CHUNKS - 1]
        s_b = s_b.reshape(batch, CTX_CHUNKS, w_rg)[:, 0]
        h0f = jnp.broadcast_to(s_f[:, None, :], (batch, SUBLANES, w_rg))
        h0b = jnp.broadcast_to(s_b[:, None, :], (batch, SUBLANES, w_rg))

        pl_ = _project(hl, w_in, N_PROJ * d, layer=l)
        (hrg_l,) = _rglru_branch(pl_, *rg_args, h0f, h0b, layer=l, chunk_len=lat_chunk, seq_chunks=LAT_CHUNKS,
                                 wrap=False, emit_state=False)
        ml = _mixer_merge(pl_, hrg_l, sc_conv_w, b_merge, w_rg_out_b, w_sc_out_b, layer=l,
                          seq_chunks=LAT_CHUNKS, wrap=False)
        xl = _mixer_oproj(ml, xl, mods, lat_row, w_o_b, **ln(1), layer=l, alpha=alpha)

        xl = ffn(xl, mods, lat_row, f2_in, f2_out, **ln(2), k0=6, emit_h=False)
        if not last:
            xc = ffn(xc, mods, ctx_row, f2_in, f2_out, **ln(2), k0=6, emit_h=False)

    return xl.reshape(batch, lat_chunk, LAT_CHUNKS, d).transpose(0, 2, 1, 3).reshape(batch, seq, d)
```

```python
import functools
import math

import jax
import jax.numpy as jnp
from jax import lax
from jax.experimental import pallas as pl
from jax.experimental.pallas import tpu as pltpu

F32 = jnp.float32
BF16 = jnp.bfloat16

SUBLANES = 8
LANES = 128
VMEM_BYTES_V7X = 64 * 1024 * 1024
VMEM_LIMIT = VMEM_BYTES_V7X - 8 * 1024 * 1024

GRID_W = 64
N_MOD = 9
RG_BLOCKS = 16
RG_CONV = 4
RG_PAD_LO = 2
SC_CONV = 3
SC_PAD_LO = 1
RG_C = 8.0
LN_EPS = 1e-5
N_PROJ = 7

SEG = GRID_W
ROW_TILE = SEG * SUBLANES
LAT_CHUNKS = SUBLANES
CTX_CHUNKS = 4
SCAN_BLOCK = 8
LN_ROWS = 128
FFN_ROW_TILE = 2 * ROW_TILE
MM_ROWS = 256
PROJ_ROW_TILE = 4 * ROW_TILE
RG_LANES = 2 * LANES


def _sigmoid(x):
    return 1.0 / (1.0 + jnp.exp(-x))


def _sigmoid_tanh(x):
    return 0.5 + 0.5 * jnp.tanh(0.5 * x)


def _gelu_tanh(x):
    return x * (0.5 * (1.0 + jnp.tanh(math.sqrt(2.0 / math.pi) * (x + 0.044715 * (x * x * x)))))


def _layer_norm(z, g, b):
    mu = jnp.mean(z, axis=-1, keepdims=True)
    zc = z - mu
    var = jnp.mean(zc * zc, axis=-1, keepdims=True)
    return zc * lax.rsqrt(var + LN_EPS) * g + b


def _params(*sem):
    return pltpu.CompilerParams(dimension_semantics=sem, vmem_limit_bytes=VMEM_LIMIT)


def _mod_kernel(c_ref, w_ref, b_ref, o_ref):
    @pl.when(pl.program_id(1) == 0)
    def _():
        o_ref[...] = jnp.broadcast_to(b_ref[...], o_ref.shape)

    c = c_ref[...]
    s = (c * _sigmoid(c)).astype(BF16)
    o_ref[...] += jnp.dot(s, w_ref[...].astype(BF16), preferred_element_type=F32)


def _mod_vectors(cond, w_mod, b_mod):
    depth, d, n = w_mod.shape
    tk = LANES
    return pl.pallas_call(
        _mod_kernel,
        grid=(depth, d // tk),
        in_specs=[
            pl.BlockSpec((SUBLANES, tk), lambda l, k: (0, k)),
            pl.BlockSpec((None, tk, n), lambda l, k: (l, k, 0)),
            pl.BlockSpec((None, 1, n), lambda l, k: (l, 0, 0)),
        ],
        out_specs=pl.BlockSpec((None, SUBLANES, n), lambda l, k: (l, 0, 0)),
        out_shape=jax.ShapeDtypeStruct((depth, SUBLANES, n), F32),
        compiler_params=_params("arbitrary", "arbitrary"),
        name="mod_vectors",
    )(cond, w_mod, b_mod.reshape(depth, 1, n))


def _ffn_kernel(x_ref, mod_ref, wg_ref, wu_ref, wo_ref, lng_ref, lnb_ref, *rest, k0, alpha, emit_h):
    if emit_h:
        o_ref, h2_ref = rest
    else:
        (o_ref,) = rest
    f = pl.program_id(1)
    tm = x_ref.shape[0]
    m = mod_ref[...]
    scale1 = 1.0 + m[k0 + 1:k0 + 2]
    shift = m[k0:k0 + 1]

    def hidden(rows):
        h = (x_ref[rows, :] * scale1 + shift).astype(BF16)
        g = jnp.dot(h, wg_ref[...], preferred_element_type=F32)
        u = jnp.dot(h, wu_ref[...], preferred_element_type=F32)
        a = (g * _sigmoid(g) * u).astype(BF16)
        return jnp.dot(a, wo_ref[...], preferred_element_type=F32)

    row_blocks = [slice(r0, r0 + MM_ROWS) for r0 in range(0, tm, MM_ROWS)]

    @pl.when(f == 0)
    def _():
        for rows in row_blocks:
            o_ref[rows, :] = hidden(rows)

    @pl.when(f > 0)
    def _():
        for rows in row_blocks:
            o_ref[rows, :] += hidden(rows)

    @pl.when(f == pl.num_programs(1) - 1)
    def _():
        half_gate = 0.5 * m[k0 + 2:k0 + 3]
        g_ln = lng_ref[...]
        b_ln = lnb_ref[...]

        def finish_rows(sb, _):
            rows = pl.ds(pl.multiple_of(sb * LN_ROWS, LN_ROWS), LN_ROWS)
            z = alpha * x_ref[rows, :] + half_gate * o_ref[rows, :]
            o_ref[rows, :] = _layer_norm(z, g_ln, b_ln)
            return 0

        lax.fori_loop(0, tm // LN_ROWS, finish_rows, 0)

        if emit_h:
            scale2 = 1.0 + m[k0 + 4:k0 + 5]
            shift2 = m[k0 + 3:k0 + 4]

            def modulate_rows(sb, _):
                rows = pl.ds(pl.multiple_of(sb * LN_ROWS, LN_ROWS), LN_ROWS)
                h2_ref[rows, :] = (o_ref[rows, :] * scale2 + shift2).astype(BF16)
                return 0

            lax.fori_loop(0, tm // LN_ROWS, modulate_rows, 0)


def _ffn_sublayer(x, mods, mod_row, w_in, w_out, ln_g, ln_b, *, layer, ln_idx, k0, alpha, emit_h):
    t, d = x.shape
    f_dim = w_out.shape[1]
    tm, tf = min(FFN_ROW_TILE, t), 512
    nf = f_dim // tf
    out_shape = [jax.ShapeDtypeStruct((t, d), F32)]
    out_specs = [pl.BlockSpec((tm, d), lambda i, f: (i, 0))]
    if emit_h:
        out_shape.append(jax.ShapeDtypeStruct((t, d), BF16))
        out_specs.append(pl.BlockSpec((tm, d), lambda i, f: (i, 0)))
    res = pl.pallas_call(
        functools.partial(_ffn_kernel, k0=k0, alpha=alpha, emit_h=emit_h),
        grid=(t // tm, nf),
        in_specs=[
            pl.BlockSpec((tm, d), lambda i, f: (i, 0)),
            pl.BlockSpec((None, None, N_MOD, d), lambda i, f: (layer, mod_row(i * tm), 0, 0)),
            pl.BlockSpec((None, d, tf), lambda i, f: (layer, 0, f)),
            pl.BlockSpec((None, d, tf), lambda i, f: (layer, 0, f + nf)),
            pl.BlockSpec((None, tf, d), lambda i, f: (layer, f, 0)),
            pl.BlockSpec((None, 1, d), lambda i, f: (ln_idx, 0, 0)),
            pl.BlockSpec((None, 1, d), lambda i, f: (ln_idx, 0, 0)),
        ],
        out_specs=out_specs,
        out_shape=out_shape,
        compiler_params=_params("arbitrary", "arbitrary"),
        name="ffn_sublayer",
    )(x, mods, w_in, w_in, w_out, ln_g, ln_b)
    return res if emit_h else res[0]


def _matmul_kernel(a_ref, b_ref, o_ref):
    o_ref[...] = jnp.dot(a_ref[...], b_ref[...].astype(BF16), preferred_element_type=F32).astype(o_ref.dtype)


def _project(h, w, n_cols, *, layer):
    t, d = h.shape
    tm = min(t, PROJ_ROW_TILE)
    tn = 1024
    return pl.pallas_call(
        _matmul_kernel,
        grid=(t // tm, n_cols // tn),
        in_specs=[
            pl.BlockSpec((tm, d), lambda i, j: (i, 0)),
            pl.BlockSpec((None, d, tn), lambda i, j: (layer, 0, j)),
        ],
        out_specs=pl.BlockSpec((tm, tn), lambda i, j: (i, j)),
        out_shape=jax.ShapeDtypeStruct((t, n_cols), BF16),
        compiler_params=_params("arbitrary", "arbitrary"),
        name="in_proj",
    )(h, w)


def _fill_halo(scr, x3, lo, hi, seq_chunks):
    n = x3.shape[0]
    scr[lo:lo + n] = x3
    tile = x3.shape[1:]
    if seq_chunks is None:
        if lo:
            scr[0:lo] = jnp.zeros((lo,) + tile, F32)
        if hi:
            scr[lo + n:lo + n + hi] = jnp.zeros((hi,) + tile, F32)
        return
    chunk = lax.broadcasted_iota(jnp.int32, tile, 0) % seq_chunks
    for t in range(lo):
        prev = pltpu.roll(x3[n - lo + t], 1, 0)
        scr[t] = jnp.where(chunk != 0, prev, 0.0)
    for t in range(hi):
        nxt = pltpu.roll(x3[t], SUBLANES - 1, 0)
        scr[lo + n + t] = jnp.where(chunk != seq_chunks - 1, nxt, 0.0)


def _conv_from_halo(scr, w, n):
    out = w[0:1] * scr[0:n]
    for k in range(1, w.shape[0]):
        out = out + w[k:k + 1] * scr[k:k + n]
    return out


def _rglru_kernel(p_ref, cw_ref, cb_ref, gw_ref, gb_ref, lam_ref, h0f_ref, h0b_ref, *rest,
                  chunk_len, seq_chunks, wrap, emit_state):
    if emit_state:
        h_ref, sf_ref, sb_ref, x_scr, a_f, v_f, a_b, v_b, h_f, p_f, h_b, p_b = rest
    else:
        h_ref, x_scr, a_f, v_f, a_b, v_b, h_f, p_f, h_b, p_b = rest
    n_seg = chunk_len // SEG
    width = p_ref.shape[1]
    cw = cw_ref[...]
    cb = cb_ref[...]
    lam = lam_ref[...]
    log_sig = jnp.minimum(lam, 0.0) - jnp.log1p(jnp.exp(-jnp.abs(lam)))
    e_fac = (0.5 * RG_C / math.log(2.0)) * log_sig
    a_scr = (a_f, a_b)
    v_scr = (v_f, v_b)

    def gates(si, _):
        r0 = pl.multiple_of(si * ROW_TILE, ROW_TILE)
        j0 = pl.multiple_of(si * SEG, SEG)
        for blk in range(width // LANES):
            cs = slice(blk * LANES, (blk + 1) * LANES)
            gw_half = gw_ref[blk] * 0.5
            gb_half = gb_ref[blk] * 0.5
            x3 = p_ref[pl.ds(r0, ROW_TILE), cs].astype(F32).reshape(SEG, SUBLANES, LANES)
            _fill_halo(x_scr, x3, RG_PAD_LO, RG_CONV - 1 - RG_PAD_LO, seq_chunks if wrap else None)
            xc = (_conv_from_halo(x_scr, cw[:, cs], SEG) + cb[:, cs]).reshape(ROW_TILE, LANES)
            t = jnp.tanh(jnp.dot(xc.astype(BF16), gw_half, preferred_element_type=F32) + gb_half)
            xh = 0.5 * xc
            for d in range(2):
                t_r = t[:, (2 * d) * LANES:(2 * d + 1) * LANES]
                t_i = t[:, (2 * d + 1) * LANES:(2 * d + 2) * LANES]
                ef = e_fac[d:d + 1, cs]
                a = jnp.exp2(ef * t_r + ef)
                y = 1.0 - a * a
                s = jnp.where(y > 0.0, y * lax.rsqrt(y), 0.0)
                v = s * (xh * t_i + xh)
                a_scr[d][pl.ds(j0, SEG), :, cs] = a.reshape(SEG, SUBLANES, LANES)
                v_scr[d][pl.ds(j0, SEG), :, cs] = v.reshape(SEG, SUBLANES, LANES)
        return 0

    lax.fori_loop(0, n_seg, gates, 0)

    def steps(blk, carry):
        hf, pf, hb, pb = carry
        jf = pl.ds(pl.multiple_of(blk * SCAN_BLOCK, SCAN_BLOCK), SCAN_BLOCK)
        jb = pl.ds(pl.multiple_of(chunk_len - SCAN_BLOCK - blk * SCAN_BLOCK, SCAN_BLOCK), SCAN_BLOCK)
        a, v = a_f[jf], v_f[jf]
        hs, ps = [], []
        for u in range(SCAN_BLOCK):
            hf = a[u] * hf + v[u]
            pf = a[u] * pf
            hs.append(hf)
            ps.append(pf)
        h_f[jf] = jnp.stack(hs)
        p_f[jf] = jnp.stack(ps)
        a, v = a_b[jb], v_b[jb]
        hs, ps = [], []
        for u in reversed(range(SCAN_BLOCK)):
            hb = a[u] * hb + v[u]
            pb = a[u] * pb
            hs.append(hb)
            ps.append(pb)
        h_b[jb] = jnp.stack(hs[::-1])
        p_b[jb] = jnp.stack(ps[::-1])
        return hf, pf, hb, pb

    zero = jnp.zeros((SUBLANES, width), F32)
    one = jnp.ones((SUBLANES, width), F32)
    hf, pf, hb, pb = lax.fori_loop(0, chunk_len // SCAN_BLOCK, steps, (zero, one, zero, one))

    sub = lax.broadcasted_iota(jnp.int32, (SUBLANES, width), 0)
    h0f = h0f_ref[...]
    h0b = h0b_ref[...]
    cf = zero
    row = None
    for s in range(SUBLANES):
        if s % seq_chunks == 0:
            row = h0f[s:s + 1]
        else:
            row = hf[s - 1:s] + pf[s - 1:s] * row
        cf = jnp.where(sub == s, row, cf)
    cbk = zero
    for s in reversed(range(SUBLANES)):
        if s % seq_chunks == seq_chunks - 1:
            row = h0b[s:s + 1]
        else:
            row = hb[s + 1:s + 2] + pb[s + 1:s + 2] * row
        cbk = jnp.where(sub == s, row, cbk)

    if emit_state:
        sf_ref[...] = hf + pf * cf
        sb_ref[...] = hb + pb * cbk

    def combine(si, _):
        j0 = pl.multiple_of(si * SEG, SEG)
        r0 = pl.multiple_of(si * ROW_TILE, ROW_TILE)
        js = pl.ds(j0, SEG)
        h = (h_f[js] + p_f[js] * cf) + (h_b[js] + p_b[js] * cbk)
        h_ref[pl.ds(r0, ROW_TILE), :] = h.reshape(ROW_TILE, width).astype(h_ref.dtype)
        return 0

    lax.fori_loop(0, n_seg, combine, 0)


def _rglru_branch(p, conv_w, conv_b, gate_w, gate_b, lam, h0f, h0b, *, layer, chunk_len, seq_chunks, wrap,
                  emit_state):
    t = p.shape[0]
    w = conv_w.shape[2]
    rows = chunk_len * SUBLANES
    groups = t // rows
    cb = RG_LANES
    n_blk = cb // LANES
    out_shape = [jax.ShapeDtypeStruct((t, w), BF16)]
    out_specs = [pl.BlockSpec((rows, cb), lambda g, c: (g, c))]
    if emit_state:
        for _ in range(2):
            out_shape.append(jax.ShapeDtypeStruct((groups, SUBLANES, w), F32))
            out_specs.append(pl.BlockSpec((None, SUBLANES, cb), lambda g, c: (g, 0, c)))
    return pl.pallas_call(
        functools.partial(_rglru_kernel, chunk_len=chunk_len, seq_chunks=seq_chunks, wrap=wrap,
                          emit_state=emit_state),
        grid=(groups, w // cb),
        in_specs=[
            pl.BlockSpec((rows, cb), lambda g, c: (g, c)),
            pl.BlockSpec((None, RG_CONV, cb), lambda g, c: (layer, 0, c)),
            pl.BlockSpec((None, 1, cb), lambda g, c: (layer, 0, c)),
            pl.BlockSpec((None, n_blk, LANES, 4 * LANES), lambda g, c: (layer, c, 0, 0)),
            pl.BlockSpec((None, n_blk, 1, 4 * LANES), lambda g, c: (layer, c, 0, 0)),
            pl.BlockSpec((None, 2, cb), lambda g, c: (layer, 0, c)),
            pl.BlockSpec((None, SUBLANES, cb), lambda g, c: (g, 0, c)),
            pl.BlockSpec((None, SUBLANES, cb), lambda g, c: (g, 0, c)),
        ],
        out_specs=out_specs,
        out_shape=out_shape,
        scratch_shapes=[pltpu.VMEM((SEG + RG_CONV - 1, SUBLANES, LANES), F32)]
        + [pltpu.VMEM((chunk_len, SUBLANES, cb), F32) for _ in range(8)],
        compiler_params=_params("arbitrary", "arbitrary"),
        name="rglru_branch",
    )(p, conv_w, conv_b, gate_w, gate_b, lam, h0f, h0b)


def _merge_kernel(hrg_ref, gate_ref, scb_ref, scc_ref, scx_ref, grg_ref, gsc_ref, scw_ref, bm_ref,
                  wrg_ref, wsc_ref, o_ref, a_scr, s_scr, q_scr, acc_rg, acc_sc, *, seq_chunks, wrap, nb):
    s = pl.program_id(0)
    last = pl.num_programs(0) - 1
    c_prev = (s + nb - 1) % nb
    tc = a_scr.shape[2]

    def produce(slot):
        a_scr[slot] = (hrg_ref[...].astype(F32) * _gelu_tanh(gate_ref[...].astype(F32))).astype(BF16)
        q = scc_ref[...].astype(F32) * scx_ref[...].astype(F32)
        _fill_halo(q_scr, q.reshape(SEG, SUBLANES, tc), SC_PAD_LO, SC_CONV - 1 - SC_PAD_LO,
                   seq_chunks if wrap else None)
        cq = _conv_from_halo(q_scr, scw_ref[...], SEG).reshape(ROW_TILE, tc)
        s_scr[slot] = (scb_ref[...].astype(F32) * cq).astype(BF16)

    def consume(slot):
        keep = c_prev != 0
        krows = pl.ds(pl.multiple_of(c_prev * tc, tc), tc)
        acc_rg[...] = (jnp.where(keep, acc_rg[...], 0.0)
                       + jnp.dot(a_scr[slot], wrg_ref[krows, :], preferred_element_type=F32))
        acc_sc[...] = (jnp.where(keep, acc_sc[...], 0.0)
                       + jnp.dot(s_scr[slot], wsc_ref[krows, :], preferred_element_type=F32))

    @pl.when(s == 0)
    def _():
        acc_rg[...] = jnp.zeros_like(acc_rg)
        acc_sc[...] = jnp.zeros_like(acc_sc)
        produce(0)

    @pl.when(jnp.logical_and(s > 0, s < last))
    def _():
        produce(s % 2)
        consume((s + 1) % 2)

    @pl.when(s == last)
    def _():
        consume((s + 1) % 2)

    @pl.when(jnp.logical_and(s > 0, c_prev == nb - 1))
    def _():
        bm = bm_ref[...]
        m = (_sigmoid_tanh(grg_ref[...].astype(F32) + bm[0:1]) * acc_rg[...]
             + _sigmoid_tanh(gsc_ref[...].astype(F32) + bm[1:2]) * acc_sc[...])
        o_ref[...] = m.astype(o_ref.dtype)


def _mixer_merge(p, h_rg, sc_conv_w, b_merge, w_rg_out, w_sc_out, *, layer, seq_chunks, wrap):
    t, d = h_rg.shape
    tm, tc = ROW_TILE, 1024
    nb = d // tc
    n_steps = (t // tm) * nb
    built = lambda s: jnp.minimum(s, n_steps - 1)
    used = lambda s: jnp.maximum(s - 1, 0)
    chunk = lambda sec: pl.BlockSpec((tm, tc), lambda s: (built(s) // nb, sec * nb + built(s) % nb))
    resident = pl.BlockSpec((None, d, d), lambda s: (layer, 0, 0), pipeline_mode=pl.Buffered(1))
    return pl.pallas_call(
        functools.partial(_merge_kernel, seq_chunks=seq_chunks, wrap=wrap, nb=nb),
        grid=(n_steps + 1,),
        in_specs=[
            chunk(0), chunk(1), chunk(2), chunk(3), chunk(4),
            pl.BlockSpec((tm, d), lambda s: (used(s) // nb, 5)),
            pl.BlockSpec((tm, d), lambda s: (used(s) // nb, 6)),
            pl.BlockSpec((None, SC_CONV, tc), lambda s: (layer, 0, built(s) % nb)),
            pl.BlockSpec((None, 2, d), lambda s: (layer, 0, 0)),
            resident, resident,
        ],
        out_specs=pl.BlockSpec((tm, d), lambda s: (used(s) // nb, 0)),
        out_shape=jax.ShapeDtypeStruct((t, d), BF16),
        scratch_shapes=[pltpu.VMEM((2, tm, tc), BF16), pltpu.VMEM((2, tm, tc), BF16),
                        pltpu.VMEM((SEG + SC_CONV - 1, SUBLANES, tc), F32),
                        pltpu.VMEM((tm, d), F32), pltpu.VMEM((tm, d), F32)],
        compiler_params=_params("arbitrary"),
        name="mixer_merge",
    )(h_rg, p, p, p, p, p, p, sc_conv_w, b_merge, w_rg_out, w_sc_out)


def _oproj_kernel(m_ref, x_ref, mod_ref, wo_ref, lng_ref, lnb_ref, o_ref, *, alpha):
    gate = mod_ref[...][5:6]
    wo = wo_ref[...]
    g_ln = lng_ref[...]
    b_ln = lnb_ref[...]
    for r0 in range(0, m_ref.shape[0], LN_ROWS):
        rows = slice(r0, r0 + LN_ROWS)
        y = jnp.dot(m_ref[rows, :], wo, preferred_element_type=F32)
        z = alpha * x_ref[rows, :] + gate * y
        o_ref[rows, :] = _layer_norm(z, g_ln, b_ln)


def _mixer_oproj(m, x, mods, mod_row, w_o, ln_g, ln_b, *, layer, ln_idx, alpha):
    t, d = x.shape
    tm = ROW_TILE
    return pl.pallas_call(
        functools.partial(_oproj_kernel, alpha=alpha),
        grid=(t // tm,),
        in_specs=[
            pl.BlockSpec((tm, d), lambda i: (i, 0)),
            pl.BlockSpec((tm, d), lambda i: (i, 0)),
            pl.BlockSpec((None, None, N_MOD, d), lambda i: (layer, mod_row(i * tm), 0, 0)),
            pl.BlockSpec((None, d, d), lambda i: (layer, 0, 0)),
            pl.BlockSpec((None, 1, d), lambda i: (ln_idx, 0, 0)),
            pl.BlockSpec((None, 1, d), lambda i: (ln_idx, 0, 0)),
        ],
        out_specs=pl.BlockSpec((tm, d), lambda i: (i, 0)),
        out_shape=jax.ShapeDtypeStruct((t, d), F32),
        compiler_params=_params("arbitrary"),
        name="mixer_oproj",
    )(m, x, mods, w_o, ln_g, ln_b)


def kernel(x, c, ctx, c_ctx, w_mod, b_mod, ln_g, ln_b, ffn1_w_in, ffn1_w_out, ffn2_w_in, ffn2_w_out, w_in, rg_conv_w, rg_conv_b, rg_gate_w, rg_gate_b, rg_lam, sc_conv_w, w_rg_out, w_sc_out, b_merge, w_o):
    batch, seq, d = x.shape
    ctx_len = ctx.shape[1]
    depth = w_mod.shape[0]
    w_rg = rg_conv_w.shape[2]
    n_ln = ln_g.shape[1]
    alpha = (2 * depth) ** 0.25
    lat_chunk = seq // LAT_CHUNKS
    ctx_chunk = ctx_len // CTX_CHUNKS
    ctx_group = SUBLANES // CTX_CHUNKS
    assert lat_chunk % GRID_W == 0 and ctx_chunk == SEG and batch % ctx_group == 0
    assert w_rg == RG_BLOCKS * LANES and d % LANES == 0

    xl = x.reshape(batch, LAT_CHUNKS, lat_chunk, d).transpose(0, 2, 1, 3).reshape(batch * seq, d)
    xc = ctx.reshape(batch // ctx_group, ctx_group, CTX_CHUNKS, ctx_chunk, d).transpose(0, 3, 1, 2, 4)
    xc = xc.reshape(batch * ctx_len, d)

    cond = jnp.zeros((SUBLANES, d), F32).at[:batch].set(c).at[batch].set(c_ctx)
    mods = _mod_vectors(cond, w_mod, b_mod).reshape(depth, SUBLANES, N_MOD, d)
    lat_row = lambda r: r // seq
    ctx_row = lambda r: batch

    bf = lambda a: a.astype(BF16)
    f1_in, f1_out, f2_in, f2_out = bf(ffn1_w_in), bf(ffn1_w_out), bf(ffn2_w_in), bf(ffn2_w_out)
    w_rg_out_b, w_sc_out_b, w_o_b = bf(w_rg_out), bf(w_sc_out), bf(w_o)
    ln_g = ln_g.reshape(depth * n_ln, 1, d)
    ln_b = ln_b.reshape(depth * n_ln, 1, d)
    gw = bf(rg_gate_w.transpose(0, 3, 4, 1, 2, 5).reshape(depth, RG_BLOCKS, LANES, 4 * LANES))
    gb = rg_gate_b.reshape(depth, 4, RG_BLOCKS, LANES).transpose(0, 2, 1, 3).reshape(depth, RG_BLOCKS, 1, 4 * LANES)
    rg_args = (rg_conv_w, rg_conv_b.reshape(depth, 1, w_rg), gw, gb, rg_lam)
    zeros_state = jnp.zeros((batch // ctx_group, SUBLANES, w_rg), F32)

    for l in range(depth):
        last = l == depth - 1
        ffn = functools.partial(_ffn_sublayer, layer=l, alpha=alpha)
        ln = lambda k: dict(ln_g=ln_g, ln_b=ln_b, ln_idx=l * n_ln + k)

        xl, hl = ffn(xl, mods, lat_row, f1_in, f1_out, **ln(0), k0=0, emit_h=True)
        xc, hc = ffn(xc, mods, ctx_row, f1_in, f1_out, **ln(0), k0=0, emit_h=True)

        pc = _project(hc, w_in, w_rg if last else N_PROJ * d, layer=l)
        hrg_c, s_f, s_b = _rglru_branch(pc, *rg_args, zeros_state, zeros_state, layer=l, chunk_len=ctx_chunk,
                                        seq_chunks=CTX_CHUNKS, wrap=True, emit_state=True)
        if not last:
            mc = _mixer_merge(pc, hrg_c, sc_conv_w, b_merge, w_rg_out_b, w_sc_out_b, layer=l,
                              seq_chunks=CTX_CHUNKS, wrap=True)
            xc = _mixer_oproj(mc, xc, mods, ctx_row, w_o_b, **ln(1), layer=l, alpha=alpha)
        s_f = s_f.reshape(batch, CTX_CHUNKS, w_rg)[:, CTX_CHUNKS - 1]
        s_b = s_b.reshape(batch, CTX_CHUNKS, w_rg)[:, 0]
        h0f = jnp.broadcast_to(s_f[:, None, :], (batch, SUBLANES, w_rg))
        h0b = jnp.broadcast_to(s_b[:, None, :], (batch, SUBLANES, w_rg))

        pl_ = _project(hl, w_in, N_PROJ * d, layer=l)
        (hrg_l,) = _rglru_branch(pl_, *rg_args, h0f, h0b, layer=l, chunk_len=lat_chunk, seq_chunks=LAT_CHUNKS,
                                 wrap=False, emit_state=False)
        ml = _mixer_merge(pl_, hrg_l, sc_conv_w, b_merge, w_rg_out_b, w_sc_out_b, layer=l,
                          seq_chunks=LAT_CHUNKS, wrap=False)
        xl = _mixer_oproj(ml, xl, mods, lat_row, w_o_b, **ln(1), layer=l, alpha=alpha)

        xl = ffn(xl, mods, lat_row, f2_in, f2_out, **ln(2), k0=6, emit_h=False)
        if not last:
            xc = ffn(xc, mods, ctx_row, f2_in, f2_out, **ln(2), k0=6, emit_h=False)

    return xl.reshape(batch, lat_chunk, LAT_CHUNKS, d).transpose(0, 2, 1, 3).reshape(batch, seq, d)
```
